```python
import functools
import jax
import jax.numpy as jnp
from jax import lax
import numpy as np

D_MODEL = 2048
BATCH = 4
SEQ = 2048
DEPTH = 2
DEC_BATCH = 32
DEC_SEQ = 4
PAST_LEN = 8192
PAGE_SIZE = 128

N_HEADS = 8
HEAD_DIM = 128
ATTN_W = N_HEADS * HEAD_DIM
Q_BLOCK = 128
CHUNK = 128
SGU_GROUPS = 8
SGU_W = 1024
SGU_GROUP_W = SGU_W // SGU_GROUPS
D_FF = 5504
N_EXPERTS = 8
TOP_K = 2
D_FF_EXPERT = 7168
N_DENSE = (DEPTH + 1) // 2
N_MOE = DEPTH // 2
SPLIT_SIZES = (ATTN_W, ATTN_W, ATTN_W, N_HEADS, SGU_W, SGU_W, D_MODEL, D_MODEL)
PROJ_W = 3 * ATTN_W + N_HEADS + 2 * SGU_W + 2 * D_MODEL
FORGET_BIAS = 6.0
CACHE_FORGET_LOGIT = 8.0
EPS = 1e-6

kernel_name = "fox_gmlp_hybrid_decode_step"


def rms_norm(x, g):
    xf = x.astype(jnp.float32)
    y = xf * lax.rsqrt(jnp.mean(xf * xf, axis=-1, keepdims=True) + EPS)
    return (y * g.astype(jnp.float32)).astype(x.dtype)


def layer_norm(x, g, b):
    xf = x.astype(jnp.float32)
    mu = jnp.mean(xf, axis=-1, keepdims=True)
    var = jnp.mean(jnp.square(xf - mu), axis=-1, keepdims=True)
    y = (xf - mu) * lax.rsqrt(var + EPS) * g.astype(jnp.float32) + b.astype(jnp.float32)
    return y.astype(x.dtype)


def ada_mod(c, w_ada, b_ada):
    m = (jax.nn.silu(c) @ w_ada + b_ada)[:, None, :]
    return jnp.split(m, 6, axis=-1)


def project_in(h, w_in, b_forget, q_norm_g, k_norm_g, sgu_ln_g, sgu_ln_b):
    z = jnp.einsum("bsd,dp->bsp", h, w_in)
    cuts = np.cumsum(SPLIT_SIZES)[:-1].tolist()
    q, k, v, f, u, vs, ga, gb = jnp.split(z, cuts, axis=-1)
    hs = z.shape[:2] + (N_HEADS, HEAD_DIM)
    q = rms_norm(q.reshape(hs), q_norm_g)
    k = rms_norm(k.reshape(hs), k_norm_g)
    v = v.reshape(hs)
    logf = jax.nn.log_sigmoid(f.astype(jnp.float32) + b_forget.astype(jnp.float32))
    u = jax.nn.gelu(u)
    vs = layer_norm(jax.nn.gelu(vs), sgu_ln_g, sgu_ln_b)
    return q, k, v, logf, u, vs, ga, gb


def fox_prompt(q, k, v, logf):
    b_, s_ = q.shape[:2]
    nb = s_ // Q_BLOCK
    c_t = jnp.cumsum(logf, axis=1).transpose(0, 2, 1)
    qb = q.reshape(b_, nb, Q_BLOCK, N_HEADS, HEAD_DIM).swapaxes(0, 1)
    cqb = c_t.reshape(b_, N_HEADS, nb, Q_BLOCK).transpose(2, 0, 1, 3)
    k_pos = jnp.arange(s_)
    scale = HEAD_DIM ** -0.5

    def block(args):
        qi, cqi, i = args
        s = jnp.einsum("bqhd,bkhd->bhqk", qi, k).astype(jnp.float32) * scale
        s = s + cqi[..., None] - c_t[:, :, None, :]
        q_pos = i * Q_BLOCK + jnp.arange(Q_BLOCK)
        s = jnp.where(k_pos[None, :] <= q_pos[:, None], s, -jnp.inf)
        p = jax.nn.softmax(s, axis=-1).astype(v.dtype)
        return jnp.einsum("bhqk,bkhd->bqhd", p, v)

    o = lax.map(block, (qb, cqb, jnp.arange(nb)))
    return o.swapaxes(0, 1).reshape(b_, s_, ATTN_W)


def fox_sample(q, k, v, logf, k_past, v_past, logf_past):
    scale = HEAD_DIM ** -0.5
    n = q.shape[1]
    cn = jnp.cumsum(logf, axis=1).transpose(0, 2, 1)
    lp = logf_past.astype(jnp.float32)
    r = (lax.cumsum(lp, axis=1, reverse=True) - lp).transpose(0, 2, 1)
    s_past = jnp.einsum("bqhd,bkhd->bhqk", q, k_past).astype(jnp.float32) * scale
    s_past = s_past + cn[..., None] + r[:, :, None, :]
    s_new = jnp.einsum("bqhd,bkhd->bhqk", q, k).astype(jnp.float32) * scale
    s_new = s_new + cn[..., None] - cn[:, :, None, :]
    causal = jnp.tril(jnp.ones((n, n), dtype=bool))
    s_new = jnp.where(causal, s_new, -jnp.inf)
    m = jnp.maximum(jnp.max(s_past, axis=-1, keepdims=True), jnp.max(s_new, axis=-1, keepdims=True))
    p_past = jnp.exp(s_past - m)
    p_new = jnp.exp(s_new - m)
    denom = (jnp.sum(p_past, axis=-1) + jnp.sum(p_new, axis=-1)).transpose(0, 2, 1)[..., None]
    o = (jnp.einsum("bhqk,bkhd->bqhd", p_past.astype(v_past.dtype), v_past).astype(jnp.float32)
         + jnp.einsum("bhqk,bkhd->bqhd", p_new.astype(v.dtype), v).astype(jnp.float32)) / denom
    return o.astype(q.dtype).reshape(q.shape[0], n, ATTN_W)


def spatial_gate(u, vs, sgu_w, sgu_b, length):
    b_, s_ = u.shape[:2]
    vg = vs.reshape(b_, s_ // length, length, SGU_GROUPS, SGU_GROUP_W)
    w = jnp.tril(sgu_w[:, :length, :length])
    mixed = jnp.einsum("gts,bnsgc->bntgc", w, vg) + sgu_b[:, :length].T[None, None, :, :, None]
    return u * mixed.reshape(b_, s_, SGU_W)


def gather_pages(cache, page_table):
    g = cache[page_table]
    return g.reshape((g.shape[0], g.shape[1] * g.shape[2]) + g.shape[3:])


def swiglu(h, w_gate, w_up, w_down):
    return (jax.nn.silu(h @ w_gate) * (h @ w_up)) @ w_down


def moe_swiglu(h, w_router, w_gate, w_up, w_down):
    b_, s_, d_ = h.shape
    t = h.reshape(-1, d_)
    logits = (t @ w_router).astype(jnp.float32)
    top_v, top_i = lax.top_k(logits, TOP_K)
    top_w = jax.nn.softmax(top_v, axis=-1)
    gates = jnp.sum(jax.nn.one_hot(top_i, N_EXPERTS, dtype=jnp.float32) * top_w[..., None], axis=1)
    out = jnp.zeros_like(t)
    for e in range(N_EXPERTS):
        out = out + gates[:, e:e + 1].astype(t.dtype) * swiglu(t, w_gate[e], w_up[e], w_down[e])
    return out.reshape(b_, s_, d_)


def mixer_sublayer(x, shift, scale, gate, attend, sgu_len, norm_g, w_in, b_forget, q_norm_g, k_norm_g,
                   sgu_ln_g, sgu_ln_b, sgu_w, sgu_b, w_branch_attn, w_branch_sgu, w_out):
    h = rms_norm(x, norm_g) * (1 + scale) + shift
    q, k, v, logf, u, vs, ga, gb = project_in(h, w_in, b_forget, q_norm_g, k_norm_g, sgu_ln_g, sgu_ln_b)
    o_attn = attend(q, k, v, logf) @ w_branch_attn
    o_sgu = spatial_gate(u, vs, sgu_w, sgu_b, sgu_len) @ w_branch_sgu
    merged = jax.nn.sigmoid(ga) * o_attn + jax.nn.sigmoid(gb) * o_sgu
    return x + gate * (merged @ w_out), k, v, logf, vs


def channel_sublayer(x, shift, scale, gate, norm_g, ffn):
    h = rms_norm(x, norm_g) * (1 + scale) + shift
    return x + gate * ffn(h)


def setup_inputs(seed: int = 0) -> dict:
    key = jax.random.key(seed)
    ks = iter(jax.random.split(key, 48))

    def nrm(shape, std):
        return jax.random.normal(next(ks), shape, jnp.float32) * std

    n_pages = PAST_LEN // PAGE_SIZE
    n_used = DEC_BATCH * n_pages
    n_pool = n_used + max(1, n_used // 4)
    page_table = jax.random.permutation(next(ks), n_pool)[:n_used].reshape(DEC_BATCH, n_pages).astype(jnp.int32)
    d = D_MODEL
    return {
        "x_prompt": nrm((BATCH, SEQ, d), 1.0),
        "x_sample": nrm((DEC_BATCH, DEC_SEQ, d), 1.0),
        "cache_k": nrm((DEPTH, n_pool, PAGE_SIZE, N_HEADS, HEAD_DIM), 1.0),
        "cache_v": nrm((DEPTH, n_pool, PAGE_SIZE, N_HEADS, HEAD_DIM), 1.0),
        "cache_logf": jax.nn.log_sigmoid(CACHE_FORGET_LOGIT + nrm((DEPTH, n_pool, PAGE_SIZE, N_HEADS), 0.5)),
        "page_table": page_table,
        "c_prompt": nrm((BATCH, d), 1.0),
        "c_sample": nrm((DEC_BATCH, d), 1.0),
        "norm1_g": 1.0 + nrm((DEPTH, d), 0.02),
        "norm2_g": 1.0 + nrm((DEPTH, d), 0.02),
        "w_ada": nrm((DEPTH, d, 6 * d), 0.5 * d ** -0.5),
        "b_ada": nrm((DEPTH, 6 * d), 0.02),
        "w_in": nrm((DEPTH, d, PROJ_W), d ** -0.5),
        "b_forget": FORGET_BIAS + nrm((DEPTH, N_HEADS), 0.1),
        "q_norm_g": 1.0 + nrm((DEPTH, HEAD_DIM), 0.02),
        "k_norm_g": 1.0 + nrm((DEPTH, HEAD_DIM), 0.02),
        "sgu_ln_g": 1.0 + nrm((DEPTH, SGU_W), 0.02),
        "sgu_ln_b": nrm((DEPTH, SGU_W), 0.02),
        "sgu_w": nrm((DEPTH, SGU_GROUPS, CHUNK, CHUNK), CHUNK ** -0.5),
        "sgu_b": 1.0 + nrm((DEPTH, SGU_GROUPS, CHUNK), 0.02),
        "w_branch_attn": nrm((DEPTH, ATTN_W, d), ATTN_W ** -0.5),
        "w_branch_sgu": nrm((DEPTH, SGU_W, d), SGU_W ** -0.5),
        "w_out": nrm((DEPTH, d, d), d ** -0.5),
        "ffd_w_gate": nrm((N_DENSE, d, D_FF), d ** -0.5),
        "ffd_w_up": nrm((N_DENSE, d, D_FF), d ** -0.5),
        "ffd_w_down": nrm((N_DENSE, D_FF, d), D_FF ** -0.5),
        "w_router": nrm((N_MOE, d, N_EXPERTS), d ** -0.5),
        "moe_w_gate": nrm((N_MOE, N_EXPERTS, d, D_FF_EXPERT), d ** -0.5),
        "moe_w_up": nrm((N_MOE, N_EXPERTS, d, D_FF_EXPERT), d ** -0.5),
        "moe_w_down": nrm((N_MOE, N_EXPERTS, D_FF_EXPERT, d), D_FF_EXPERT ** -0.5),
    }


def reference(x_prompt, x_sample, cache_k, cache_v, cache_logf, page_table, c_prompt, c_sample,
              norm1_g, norm2_g, w_ada, b_ada, w_in, b_forget, q_norm_g, k_norm_g, sgu_ln_g, sgu_ln_b,
              sgu_w, sgu_b, w_branch_attn, w_branch_sgu, w_out, ffd_w_gate, ffd_w_up, ffd_w_down,
              w_router, moe_w_gate, moe_w_up, moe_w_down):
    xp, xs = x_prompt, x_sample
    kp_l, vp_l, fp_l, ks_l, vs_l, fs_l, ss_l = [], [], [], [], [], [], []
    for l in range(DEPTH):
        mix_w = (norm1_g[l], w_in[l], b_forget[l], q_norm_g[l], k_norm_g[l], sgu_ln_g[l], sgu_ln_b[l],
                 sgu_w[l], sgu_b[l], w_branch_attn[l], w_branch_sgu[l], w_out[l])
        j = l // 2
        if l % 2 == 0:
            ffn = functools.partial(swiglu, w_gate=ffd_w_gate[j], w_up=ffd_w_up[j], w_down=ffd_w_down[j])
        else:
            ffn = functools.partial(moe_swiglu, w_router=w_router[j], w_gate=moe_w_gate[j],
                                    w_up=moe_w_up[j], w_down=moe_w_down[j])
        sh1, sc1, g1, sh2, sc2, g2 = ada_mod(c_prompt, w_ada[l], b_ada[l])
        xp, k_new, v_new, lf_new, _ = mixer_sublayer(xp, sh1, sc1, g1, fox_prompt, CHUNK, *mix_w)
        xp = channel_sublayer(xp, sh2, sc2, g2, norm2_g[l], ffn)
        kp_l.append(k_new)
        vp_l.append(v_new)
        fp_l.append(lf_new)
        attend = functools.partial(fox_sample,
                                   k_past=gather_pages(cache_k[l], page_table),
                                   v_past=gather_pages(cache_v[l], page_table),
                                   logf_past=gather_pages(cache_logf[l], page_table))
        sh1, sc1, g1, sh2, sc2, g2 = ada_mod(c_sample, w_ada[l], b_ada[l])
        xs, k_new, v_new, lf_new, sgu_v_new = mixer_sublayer(xs, sh1, sc1, g1, attend, xs.shape[1], *mix_w)
        xs = channel_sublayer(xs, sh2, sc2, g2, norm2_g[l], ffn)
        ks_l.append(k_new)
        vs_l.append(v_new)
        fs_l.append(lf_new)
        ss_l.append(sgu_v_new)
    new_k_prompt = jnp.stack(kp_l)
    new_v_prompt = jnp.stack(vp_l)
    new_logf_prompt = jnp.stack(fp_l)
    new_k_sample = jnp.stack(ks_l)
    new_v_sample = jnp.stack(vs_l)
    new_logf_sample = jnp.stack(fs_l)
    new_sgu_v_sample = jnp.stack(ss_l)
    return (xp, xs, new_k_prompt, new_v_prompt, new_logf_prompt, new_k_sample, new_v_sample, new_logf_sample, new_sgu_v_sample)
```

```python
import functools

import jax
import jax.numpy as jnp
from jax import lax
from jax.experimental import pallas as pl
from jax.experimental.pallas import tpu as pltpu

F32, BF16, I32 = jnp.float32, jnp.bfloat16, jnp.int32

D = 2048
NH, DH = 8, 128
AW = NH * DH
SW = 1024
NG, GW = 8, 128
CH = 128
NE = 8
EPS = 1e-6
NEG = -1e30

MOE_CHUNK = 1280
MOE_SUB = 256
PAGES_PER_STEP = 8


def _cp(sem, mb):
    return pltpu.CompilerParams(dimension_semantics=sem, vmem_limit_bytes=mb << 20)


def _normmod(x, g, sc, sh):
    y = x * lax.rsqrt(jnp.mean(x * x, axis=-1, keepdims=True) + EPS)
    return (y * g) * (1.0 + sc) + sh


def _split3(x):
    x1 = x.astype(BF16)
    r1 = x - x1.astype(F32)
    x2 = r1.astype(BF16)
    r2 = r1 - x2.astype(F32)
    return x1, x2, r2.astype(BF16)


def _dot3(x, m):
    return sum(jnp.dot(p, m, preferred_element_type=F32) for p in _split3(x))


def _dot_nt(a, b):
    return lax.dot_general(a, b, (((1,), (1,)), ((), ())), preferred_element_type=F32)


def _silu(x):
    return x * jax.nn.sigmoid(x)


def _mod_spec(tm, rows_per_group, r):
    return pl.BlockSpec((None, r, D), lambda i, *_: ((i * tm) // rows_per_group, 0, 0))


def _ada_kernel(c_ref, w_ref, b_ref, o_ref):
    a = _silu(c_ref[...]).astype(BF16)
    o_ref[...] = jnp.dot(a, w_ref[...].astype(BF16), preferred_element_type=F32) + b_ref[...]


def _ada(c_all, w_ada, b_ada):
    nl, _, n = w_ada.shape
    mc = c_all.shape[0]
    tn = 1024
    return pl.pallas_call(
        _ada_kernel,
        grid=(nl, n // tn),
        in_specs=[pl.BlockSpec((mc, D), lambda l, j: (0, 0)),
                  pl.BlockSpec((None, D, tn), lambda l, j: (l, 0, j)),
                  pl.BlockSpec((None, 1, tn), lambda l, j: (l, 0, j))],
        out_specs=pl.BlockSpec((None, mc, tn), lambda l, j: (l, 0, j)),
        out_shape=jax.ShapeDtypeStruct((nl, mc, n), F32),
        compiler_params=_cp(("parallel", "parallel"), 40),
        name="ada",
    )(c_all, w_ada, b_ada.reshape(nl, 1, n))


_TN_IN = 512
_J_Q, _J_K, _J_V, _J_U, _J_VS, _J_GA, _J_GB, _J_END = 0, 2, 4, 6, 8, 10, 14, 18


def _headnorm(z, g):
    outs = []
    for hh in range(z.shape[1] // DH):
        zz = z[:, hh * DH:(hh + 1) * DH]
        outs.append(zz * lax.rsqrt(jnp.mean(zz * zz, axis=-1, keepdims=True) + EPS) * g)
    return jnp.concatenate(outs, axis=-1)


def _inproj_kernel(x_ref, g_ref, sc_ref, sh_ref, w_ref, wf_ref, bf_ref, qg_ref, kg_ref, lng_ref, lnb_ref,
                   q_ref, k_ref, v_ref, lf_ref, u_ref, vs_ref, ga_ref, gb_ref, h_scr):
    j = pl.program_id(1)

    @pl.when(j == 0)
    def _():
        h = _normmod(x_ref[...], g_ref[...], sc_ref[...], sh_ref[...]).astype(BF16)
        h_scr[...] = h
        f = jnp.dot(h, wf_ref[...], preferred_element_type=F32) + bf_ref[...]
        lf_ref[...] = jnp.minimum(f, 0.0) - jnp.log1p(jnp.exp(-jnp.abs(f)))

    z = jnp.dot(h_scr[...], w_ref[...], preferred_element_type=F32)

    @pl.when(j < _J_K)
    def _():
        q_ref[...] = (_headnorm(z, qg_ref[...]) * (DH ** -0.5)).astype(BF16)

    @pl.when((j >= _J_K) & (j < _J_V))
    def _():
        k_ref[...] = _headnorm(z, kg_ref[...])

    @pl.when((j >= _J_V) & (j < _J_U))
    def _():
        v_ref[...] = z

    @pl.when((j >= _J_U) & (j < _J_VS))
    def _():
        u_ref[...] = jax.nn.gelu(z)

    @pl.when(j == _J_VS)
    def _():
        vs_ref[:, :_TN_IN] = jax.nn.gelu(z)

    @pl.when(j == _J_VS + 1)
    def _():
        vs_ref[:, _TN_IN:] = jax.nn.gelu(z)
        t = vs_ref[...]
        mu = jnp.mean(t, axis=-1, keepdims=True)
        tc = t - mu
        var = jnp.mean(tc * tc, axis=-1, keepdims=True)
        vs_ref[...] = tc * lax.rsqrt(var + EPS) * lng_ref[...] + lnb_ref[...]

    @pl.when((j >= _J_GA) & (j < _J_GB))
    def _():
        ga_ref[...] = jax.nn.sigmoid(z)

    @pl.when(j >= _J_GB)
    def _():
        gb_ref[...] = jax.nn.sigmoid(z)


def _inproj(x2d, g, sc, sh, rpg, r, wp, wf, bf, qg, kg, lng, lnb, tm):
    t = x2d.shape[0]
    tn = _TN_IN

    def cspec(j0, nb, width=tn):
        return pl.BlockSpec((tm, width), lambda i, j: (i, jnp.clip(j - j0, 0, nb - 1)))

    full = lambda a: pl.BlockSpec(a.shape, lambda i, j: (0,) * a.ndim)
    return pl.pallas_call(
        _inproj_kernel,
        grid=(t // tm, _J_END),
        in_specs=[pl.BlockSpec((tm, D), lambda i, j: (i, 0)), full(g), _mod_spec(tm, rpg, r), _mod_spec(tm, rpg, r),
                  pl.BlockSpec((D, tn), lambda i, j: (0, j)), full(wf), full(bf), full(qg), full(kg), full(lng),
                  full(lnb)],
        out_specs=[cspec(_J_Q, 2), cspec(_J_K, 2), cspec(_J_V, 2),
                   pl.BlockSpec((tm, 128), lambda i, j: (i, 0)),
                   cspec(_J_U, 2),
                   pl.BlockSpec((tm, SW), lambda i, j: (i, 0)),
                   cspec(_J_GA, 4), cspec(_J_GB, 4)],
        out_shape=[jax.ShapeDtypeStruct((t, AW), BF16), jax.ShapeDtypeStruct((t, AW), F32),
                   jax.ShapeDtypeStruct((t, AW), F32), jax.ShapeDtypeStruct((t, 128), F32),
                   jax.ShapeDtypeStruct((t, SW), F32), jax.ShapeDtypeStruct((t, SW), F32),
                   jax.ShapeDtypeStruct((t, D), F32), jax.ShapeDtypeStruct((t, D), F32)],
        scratch_shapes=[pltpu.VMEM((tm, D), BF16)],
        compiler_params=_cp(("parallel", "arbitrary"), 48),
        name="inproj",
    )(x2d, g, sc, sh, wp, wf, bf, qg, kg, lng, lnb)


def _cumsum_kernel(x_ref, o_ref):
    s = x_ref.shape[-1]
    r = lax.broadcasted_iota(I32, (128, 128), 0)
    c = lax.broadcasted_iota(I32, (128, 128), 1)
    tri = jnp.where(r <= c, 1.0, 0.0).astype(BF16)
    carry = jnp.zeros((NH, 1), F32)
    for b in range(s // 128):
        y = _dot3(x_ref[:, b * 128:(b + 1) * 128], tri) + carry
        o_ref[:, b * 128:(b + 1) * 128] = y
        carry = y[:, 127:128]


def _cumsum(lft):
    b, _, s = lft.shape
    return pl.pallas_call(
        _cumsum_kernel,
        grid=(b,),
        in_specs=[pl.BlockSpec((None, NH, s), lambda i: (i, 0, 0))],
        out_specs=pl.BlockSpec((None, NH, s), lambda i: (i, 0, 0)),
        out_shape=jax.ShapeDtypeStruct((b, NH, s), F32),
        compiler_params=_cp(("parallel",), 32),
        name="logf_cumsum",
    )(lft)


def _flash_kernel(q_ref, k_ref, v_ref, cc_ref, cr_ref, o_ref, *, tq):
    h = pl.program_id(1)
    i = pl.program_id(2)
    q = q_ref[...]
    lane = lax.broadcasted_iota(I32, (tq, NH), 1)
    cq = jnp.sum(jnp.where(lane == h, cc_ref[...], 0.0), axis=-1, keepdims=True)

    def step(kb, carry, masked):
        m, l, acc = carry
        k0 = pl.multiple_of(kb * tq, tq)
        k = k_ref[pl.ds(k0, tq), :].astype(BF16)
        v = v_ref[pl.ds(k0, tq), :].astype(BF16)
        ck = cr_ref[pl.ds(h, 1), pl.ds(k0, tq)]
        s = _dot_nt(q, k) + cq - ck
        if masked:
            row = lax.broadcasted_iota(I32, (tq, tq), 0)
            col = lax.broadcasted_iota(I32, (tq, tq), 1)
            s = jnp.where(col <= row, s, NEG)
        m_new = jnp.maximum(m, jnp.max(s, axis=-1, keepdims=True))
        alpha = jnp.exp(m - m_new)
        p = jnp.exp(s - m_new)
        l = alpha * l + jnp.sum(p, axis=-1, keepdims=True)
        acc = alpha * acc + jnp.dot(p.astype(BF16), v, preferred_element_type=F32)
        return m_new, l, acc

    init = (jnp.full((tq, 1), NEG, F32), jnp.zeros((tq, 1), F32), jnp.zeros((tq, DH), F32))
    carry = lax.fori_loop(0, i, lambda kb, c: step(kb, c, False), init)
    _, l, acc = step(i, carry, True)
    o_ref[...] = (acc / l).astype(BF16)


def _flash(q, k, v, ccol, crow, tq=256):
    b, s, _ = q.shape
    return pl.pallas_call(
        functools.partial(_flash_kernel, tq=tq),
        grid=(b, NH, s // tq),
        in_specs=[pl.BlockSpec((None, tq, DH), lambda bb, h, i: (bb, i, h)),
                  pl.BlockSpec((None, s, DH), lambda bb, h, i: (bb, 0, h)),
                  pl.BlockSpec((None, s, DH), lambda bb, h, i: (bb, 0, h)),
                  pl.BlockSpec((None, tq, NH), lambda bb, h, i: (bb, i, 0)),
                  pl.BlockSpec((None, NH, s), lambda bb, h, i: (bb, 0, 0))],
        out_specs=pl.BlockSpec((None, tq, DH), lambda bb, h, i: (bb, i, h)),
        out_shape=jax.ShapeDtypeStruct((b, s, AW), BF16),
        compiler_params=_cp(("parallel", "parallel", "arbitrary"), 32),
        name="fox_prompt",
    )(q, k, v, ccol, crow)


def _logf_pages_kernel(x_ref, ms_ref, mt_ref, r_ref, t_ref):
    p1, p2, p3 = _split3(x_ref[...])
    ms, mt = ms_ref[...], mt_ref[...]
    dot = lambda a, m: jnp.dot(a, m, preferred_element_type=F32)
    r_ref[...] = dot(p1, ms) + dot(p2, ms) + dot(p3, ms)
    t_ref[...] = dot(p1, mt) + dot(p2, mt) + dot(p3, mt)


def _logf_pages(lf_flat):
    nl, npool, w = lf_flat.shape
    tp = 256
    a = jnp.arange(w)
    same_head = (a[:, None] % NH) == (a[None, :] % NH)
    ms = (same_head & (a[:, None] // NH > a[None, :] // NH)).astype(BF16)
    mt = same_head.astype(BF16)
    cm = pl.BlockSpec((w, w), lambda l, i: (0, 0))
    blk = pl.BlockSpec((None, tp, w), lambda l, i: (l, i, 0))
    return pl.pallas_call(
        _logf_pages_kernel,
        grid=(nl, npool // tp),
        in_specs=[blk, cm, cm],
        out_specs=[blk, blk],
        out_shape=[jax.ShapeDtypeStruct((nl, npool, w), F32)] * 2,
        compiler_params=_cp(("parallel", "parallel"), 32),
        name="logf_pages",
    )(lf_flat, ms, mt)


def _paged_kernel(pt_ref, q_ref, kn_ref, vn_ref, lfn_ref, *refs, g, npages):
    ks, vs, rl, tt = refs[:g], refs[g:2 * g], refs[2 * g:3 * g], refs[3 * g:4 * g]
    o_ref = refs[4 * g]
    m_scr, l_scr, acc_scr, carry_scr, cn_scr = refs[4 * g + 1:]
    b = pl.program_id(0)
    s = pl.program_id(1)
    ns = pl.num_programs(1)
    nq = 4 * NH
    w = CH * NH
    q = q_ref[...]

    @pl.when(s == 0)
    def _():
        r = lax.broadcasted_iota(I32, (nq, nq), 0)
        c = lax.broadcasted_iota(I32, (nq, nq), 1)
        lf = lfn_ref[...]
        m_kh = jnp.where((r % NH == c % NH) & (r // NH <= c // NH), 1.0, 0.0).astype(BF16)
        m_hq = jnp.where((r % NH == c // 4) & (r // NH <= c % 4), 1.0, 0.0).astype(BF16)
        cn_kh = _dot3(lf, m_kh)
        cn_hq = _dot3(lf, m_hq)
        cn_col = jnp.sum(jnp.where(r == c, jnp.broadcast_to(cn_hq, (nq, nq)), 0.0), axis=1, keepdims=True)
        sn = _dot_nt(q, kn_ref[...].astype(BF16)) + cn_col - cn_kh
        sn = jnp.where((r // 4 == c % NH) & (c // NH <= r % 4), sn, NEG)
        m = jnp.max(sn, axis=-1, keepdims=True)
        p = jnp.exp(sn - m)
        m_scr[...] = jnp.broadcast_to(m, m_scr.shape)
        l_scr[...] = jnp.broadcast_to(jnp.sum(p, axis=-1, keepdims=True), l_scr.shape)
        acc_scr[...] = jnp.dot(p.astype(BF16), vn_ref[...].astype(BF16), preferred_element_type=F32)
        cn_scr[...] = jnp.broadcast_to(cn_col, cn_scr.shape)
        carry_scr[...] = jnp.zeros_like(carry_scr)

    row = lax.broadcasted_iota(I32, (nq, w), 0)
    col = lax.broadcasted_iota(I32, (nq, w), 1)
    head_ok = (row // 4) == (col % NH)
    cn_col = cn_scr[:, 0:1]
    m, l, acc, carry = m_scr[:, 0:1], l_scr[:, 0:1], acc_scr[...], carry_scr[...]
    for gi in reversed(range(g)):
        page = pt_ref[b * npages + (ns - 1 - s) * g + gi]
        sub = page % 8
        bias = rl[gi][pl.ds(sub, 1), :] + carry
        carry = carry + tt[gi][pl.ds(sub, 1), :]
        st = _dot_nt(q, ks[gi][...].astype(BF16)) + bias + cn_col
        st = jnp.where(head_ok, st, NEG)
        m_new = jnp.maximum(m, jnp.max(st, axis=-1, keepdims=True))
        alpha = jnp.exp(m - m_new)
        p = jnp.exp(st - m_new)
        l = alpha * l + jnp.sum(p, axis=-1, keepdims=True)
        acc = alpha * acc + jnp.dot(p.astype(BF16), vs[gi][...].astype(BF16), preferred_element_type=F32)
        m = m_new
    m_scr[...] = jnp.broadcast_to(m, m_scr.shape)
    l_scr[...] = jnp.broadcast_to(l, l_scr.shape)
    acc_scr[...] = acc
    carry_scr[...] = carry

    @pl.when(s == ns - 1)
    def _():
        o_ref[...] = (acc / l).astype(BF16)


def _paged(layer, page_table, q_hq, kn, vn, lfn, cache_k2, cache_v2, rloc, tot):
    db, nq, _ = q_hq.shape
    npages = page_table.shape[1]
    g = PAGES_PER_STEP
    ns = npages // g
    w = CH * NH

    def page_of(b, s, pt, gi):
        return pt[b * npages + (ns - 1 - s) * g + gi]

    per_seq = lambda shp: pl.BlockSpec((None,) + shp, lambda b, s, pt: (b, 0, 0))
    kv_specs = [pl.BlockSpec((None, None, w, DH), lambda b, s, pt, gi=gi: (layer, page_of(b, s, pt, gi), 0, 0))
                for gi in range(g)]
    lf_specs = [pl.BlockSpec((None, 8, w), lambda b, s, pt, gi=gi: (layer, page_of(b, s, pt, gi) // 8, 0))
                for gi in range(g)]
    grid_spec = pltpu.PrefetchScalarGridSpec(
        num_scalar_prefetch=1,
        grid=(db, ns),
        in_specs=[per_seq((nq, DH)), per_seq((nq, DH)), per_seq((nq, DH)), per_seq((1, nq))]
        + kv_specs + kv_specs + lf_specs + lf_specs,
        out_specs=per_seq((nq, DH)),
        scratch_shapes=[pltpu.VMEM((nq, 128), F32), pltpu.VMEM((nq, 128), F32), pltpu.VMEM((nq, DH), F32),
                        pltpu.VMEM((1, w), F32), pltpu.VMEM((nq, 128), F32)],
    )
    return pl.pallas_call(
        functools.partial(_paged_kernel, g=g, npages=npages),
        grid_spec=grid_spec,
        out_shape=jax.ShapeDtypeStruct((db, nq, DH), BF16),
        compiler_params=_cp(("parallel", "arbitrary"), 48),
        name="fox_decode",
    )(page_table.reshape(-1), q_hq, kn, vn, lfn, *([cache_k2] * g), *([cache_v2] * g), *([rloc] * g), *([tot] * g))


def _sgu_kernel(u_ref, vs_ref, w_ref, b_ref, o_ref):
    tm = u_ref.shape[0]
    r = lax.broadcasted_iota(I32, (CH, CH), 0)
    c = lax.broadcasted_iota(I32, (CH, CH), 1)
    for gi in range(NG):
        wg = jnp.where(c <= r, w_ref[gi], 0.0).astype(BF16)
        bg = b_ref[gi]
        cs = slice(gi * GW, (gi + 1) * GW)
        for ci in range(tm // CH):
            rs = slice(ci * CH, (ci + 1) * CH)
            mixed = jnp.dot(wg, vs_ref[rs, cs].astype(BF16), preferred_element_type=F32) + bg
            o_ref[rs, cs] = (u_ref[rs, cs] * mixed).astype(BF16)


def _sgu(u, vs, wmix, bmat, tm):
    t = u.shape[0]
    blk = pl.BlockSpec((tm, SW), lambda i: (i, 0))
    par = pl.BlockSpec((NG, CH, CH), lambda i: (0, 0, 0))
    return pl.pallas_call(
        _sgu_kernel,
        grid=(t // tm,),
        in_specs=[blk, blk, par, par],
        out_specs=blk,
        out_shape=jax.ShapeDtypeStruct((t, SW), BF16),
        compiler_params=_cp(("parallel",), 32),
        name="sgu",
    )(u, vs, wmix, bmat)


def _merge_kernel(oa_ref, sg_ref, wa_ref, ws_ref, ga_ref, gb_ref, o_ref):
    a = jnp.dot(oa_ref[...], wa_ref[...].astype(BF16), preferred_element_type=F32)
    s = jnp.dot(sg_ref[...], ws_ref[...].astype(BF16), preferred_element_type=F32)
    o_ref[...] = (ga_ref[...] * a + gb_ref[...] * s).astype(BF16)


def _merge(oa, sg, wa, ws, ga, gb, tm, tn=512):
    t = oa.shape[0]
    a_spec = pl.BlockSpec((tm, AW), lambda i, j: (i, 0))
    w_spec = pl.BlockSpec((AW, tn), lambda i, j: (0, j))
    o_spec = pl.BlockSpec((tm, tn), lambda i, j: (i, j))
    return pl.pallas_call(
        _merge_kernel,
        grid=(t // tm, D // tn),
        in_specs=[a_spec, a_spec, w_spec, w_spec, o_spec, o_spec],
        out_specs=o_spec,
        out_shape=jax.ShapeDtypeStruct((t, D), BF16),
        compiler_params=_cp(("parallel", "parallel"), 48),
        name="merge",
    )(oa, sg, wa, ws, ga, gb)


def _outproj_kernel(a_ref, w_ref, x_ref, g_ref, o_ref):
    y = jnp.dot(a_ref[...], w_ref[...].astype(BF16), preferred_element_type=F32)
    o_ref[...] = x_ref[...] + g_ref[...] * y


def _outproj(a, w, x2d, gate, rpg, r, tm, tn=512):
    t = a.shape[0]
    return pl.pallas_call(
        _outproj_kernel,
        grid=(t // tm, D // tn),
        in_specs=[pl.BlockSpec((tm, D), lambda i, j: (i, 0)),
                  pl.BlockSpec((D, tn), lambda i, j: (0, j)),
                  pl.BlockSpec((tm, tn), lambda i, j: (i, j)),
                  pl.BlockSpec((None, r, tn), lambda i, j: ((i * tm) // rpg, 0, j))],
        out_specs=pl.BlockSpec((tm, tn), lambda i, j: (i, j)),
        out_shape=jax.ShapeDtypeStruct((t, D), F32),
        compiler_params=_cp(("parallel", "parallel"), 48),
        name="outproj",
    )(a, w, x2d, gate)


def _ffn_kernel(x_ref, g_ref, sc_ref, sh_ref, gate_ref, wg_ref, wu_ref, wd_ref, o_ref, h_scr, *, dff, tf):
    f = pl.program_id(1)

    @pl.when(f == 0)
    def _():
        h_scr[...] = _normmod(x_ref[...], g_ref[...], sc_ref[...], sh_ref[...]).astype(BF16)
        o_ref[...] = jnp.zeros_like(o_ref)

    h = h_scr[...]
    col_ok = f * tf + lax.broadcasted_iota(I32, (1, tf), 1) < dff
    row_ok = f * tf + lax.broadcasted_iota(I32, (tf, 1), 0) < dff
    gg = jnp.dot(h, wg_ref[...].astype(BF16), preferred_element_type=F32)
    uu = jnp.dot(h, wu_ref[...].astype(BF16), preferred_element_type=F32)
    a = jnp.where(col_ok, _silu(gg) * uu, 0.0).astype(BF16)
    wd = jnp.where(row_ok, wd_ref[...], 0.0).astype(BF16)
    o_ref[...] += jnp.dot(a, wd, preferred_element_type=F32)

    @pl.when(f == pl.num_programs(1) - 1)
    def _():
        o_ref[...] = x_ref[...] + gate_ref[...] * o_ref[...]


def _ffn_dense(x2d, g, sc, sh, gate, rpg, r, wg, wu, wd, tm, tf=256):
    t = x2d.shape[0]
    dff = wg.shape[1]
    full = lambda a: pl.BlockSpec(a.shape, lambda i, j: (0,) * a.ndim)
    ms = _mod_spec(tm, rpg, r)
    return pl.pallas_call(
        functools.partial(_ffn_kernel, dff=dff, tf=tf),
        grid=(t // tm, pl.cdiv(dff, tf)),
        in_specs=[pl.BlockSpec((tm, D), lambda i, j: (i, 0)), full(g), ms, ms, ms,
                  pl.BlockSpec((D, tf), lambda i, j: (0, j)),
                  pl.BlockSpec((D, tf), lambda i, j: (0, j)),
                  pl.BlockSpec((tf, D), lambda i, j: (j, 0))],
        out_specs=pl.BlockSpec((tm, D), lambda i, j: (i, 0)),
        out_shape=jax.ShapeDtypeStruct((t, D), F32),
        scratch_shapes=[pltpu.VMEM((tm, D), BF16)],
        compiler_params=_cp(("parallel", "arbitrary"), 52),
        name="ffn_dense",
    )(x2d, g, sc, sh, gate, wg, wu, wd)


def _router_kernel(x_ref, g_ref, sc_ref, sh_ref, wr_ref, h_ref, idx_ref, wt_ref):
    h = _normmod(x_ref[...], g_ref[...], sc_ref[...], sh_ref[...])
    h_ref[...] = h
    logits = jnp.dot(h.astype(BF16), wr_ref[...], preferred_element_type=F32)
    ls = [logits[:, e:e + 1] for e in range(NE)]
    m1 = functools.reduce(jnp.maximum, ls)
    i1 = jnp.full(m1.shape, NE, I32)
    for e in reversed(range(NE)):
        i1 = jnp.where(ls[e] == m1, e, i1)
    ls2 = [jnp.where(i1 == e, -jnp.inf, ls[e]) for e in range(NE)]
    m2 = functools.reduce(jnp.maximum, ls2)
    i2 = jnp.full(m1.shape, NE, I32)
    for e in reversed(range(NE)):
        i2 = jnp.where(ls2[e] == m2, e, i2)
    e2 = jnp.exp(m2 - m1)
    p1 = 1.0 / (1.0 + e2)
    lane = lax.broadcasted_iota(I32, idx_ref.shape, 1)
    idx_ref[...] = jnp.where(lane == 0, i1, jnp.where(lane == 1, i2, 0))
    wt_ref[...] = jnp.where(lane == 0, p1, jnp.where(lane == 1, e2 * p1, 0.0))


def _router(x2d, g, sc, sh, rpg, r, wr_t, tm):
    t = x2d.shape[0]
    full = lambda a: pl.BlockSpec(a.shape, lambda i: (0,) * a.ndim)
    ms = _mod_spec(tm, rpg, r)
    return pl.pallas_call(
        _router_kernel,
        grid=(t // tm,),
        in_specs=[pl.BlockSpec((tm, D), lambda i: (i, 0)), full(g), ms, ms, full(wr_t)],
        out_specs=[pl.BlockSpec((tm, D), lambda i: (i, 0)), pl.BlockSpec((tm, 128), lambda i: (i, 0)),
                   pl.BlockSpec((tm, 128), lambda i: (i, 0))],
        out_shape=[jax.ShapeDtypeStruct((t, D), F32), jax.ShapeDtypeStruct((t, 128), I32),
                   jax.ShapeDtypeStruct((t, 128), F32)],
        compiler_params=_cp(("parallel",), 32),
        name="router",
    )(x2d, g, sc, sh, wr_t)


def _moe_plan(idx2, nc):
    r = MOE_CHUNK
    e_flat = idx2.reshape(-1)
    npairs = e_flat.shape[0]
    onehot = (e_flat[:, None] == jnp.arange(NE, dtype=I32)[None, :]).astype(I32)
    csum = jnp.cumsum(onehot, axis=0)
    rank = jnp.sum(onehot * csum, axis=1) - 1
    counts = csum[-1]
    nch = (counts + r - 1) // r
    cstart = jnp.cumsum(nch) - nch
    n_used = jnp.sum(nch)
    chunk = jnp.sum(onehot * cstart[None, :], axis=1) + rank // r
    pos = chunk * r + rank % r
    src = jnp.zeros((nc * r,), I32).at[pos].set(jnp.arange(npairs, dtype=I32) // 2)
    cid = jnp.minimum(jnp.arange(nc, dtype=I32), n_used - 1)
    ce = jnp.sum((cid[:, None] >= cstart[None, :]).astype(I32), axis=1) - 1
    rows = jnp.clip(counts[ce] - (cid - cstart[ce]) * r, 0, r)
    rows = jnp.where(jnp.arange(nc) < n_used, rows, 0)
    nsub = (rows + MOE_SUB - 1) // MOE_SUB
    return pos, src, cid, ce, nsub


def _gather_kernel(src_ref, nsub_ref, h_hbm, o_ref, buf, sem):
    s = pl.program_id(0)
    per_chunk = MOE_CHUNK // MOE_SUB
    live = (s % per_chunk) < nsub_ref[s // per_chunk]

    def row_copy(r, t):
        return pltpu.make_async_copy(h_hbm.at[pl.ds(t, 1)], buf.at[pl.ds(r, 1)], sem)

    @pl.when(live)
    def _():
        def issue(r, c):
            row_copy(r, src_ref[s * MOE_SUB + r]).start()
            return c

        lax.fori_loop(0, MOE_SUB, issue, 0)

        def drain(r, c):
            row_copy(r, 0).wait()
            return c

        lax.fori_loop(0, MOE_SUB, drain, 0)
        o_ref[...] = buf[...].astype(BF16)

    @pl.when(jnp.logical_not(live))
    def _():
        o_ref[...] = jnp.zeros_like(o_ref)


def _gather_rows(h, src, nsub, nc):
    nrows = nc * MOE_CHUNK
    grid_spec = pltpu.PrefetchScalarGridSpec(
        num_scalar_prefetch=2,
        grid=(nrows // MOE_SUB,),
        in_specs=[pl.BlockSpec(memory_space=pl.ANY)],
        out_specs=pl.BlockSpec((MOE_SUB, D), lambda s, *_: (s, 0)),
        scratch_shapes=[pltpu.VMEM((MOE_SUB, D), F32), pltpu.SemaphoreType.DMA(())],
    )
    return pl.pallas_call(
        _gather_kernel,
        grid_spec=grid_spec,
        out_shape=jax.ShapeDtypeStruct((nrows, D), BF16),
        compiler_params=_cp(("arbitrary",), 32),
        name="moe_gather",
    )(src, nsub, h)


def _moe_ffn_kernel(cid_ref, ce_ref, nsub_ref, hs_ref, wg_ref, wu_ref, wd_ref, y_ref, wgb, wub, wdb):
    c = pl.program_id(0)
    f = pl.program_id(1)
    nsub = nsub_ref[c]

    @pl.when(f == 0)
    def _():
        y_ref[...] = jnp.zeros_like(y_ref)

    @pl.when(nsub > 0)
    def _():
        wgb[...] = wg_ref[...].astype(BF16)
        wub[...] = wu_ref[...].astype(BF16)
        wdb[...] = wd_ref[...].astype(BF16)

        def body(s, carry):
            r0 = pl.multiple_of(s * MOE_SUB, MOE_SUB)
            h = hs_ref[pl.ds(r0, MOE_SUB), :]
            gg = jnp.dot(h, wgb[...], preferred_element_type=F32)
            uu = jnp.dot(h, wub[...], preferred_element_type=F32)
            a = (_silu(gg) * uu).astype(BF16)
            y_ref[pl.ds(r0, MOE_SUB), :] += jnp.dot(a, wdb[...], preferred_element_type=F32)
            return carry

        lax.fori_loop(0, nsub, body, 0)


def _moe_ffn(hs, cid, ce, nsub, wg, wu, wd, nc, tf=256):
    dffe = wg.shape[2]
    nf = dffe // tf
    r = MOE_CHUNK

    def fidx(c, f, nsub):
        return jnp.where(nsub[c] > 0, f, nf - 1)

    grid_spec = pltpu.PrefetchScalarGridSpec(
        num_scalar_prefetch=3,
        grid=(nc, nf),
        in_specs=[pl.BlockSpec((r, D), lambda c, f, cid, ce, ns: (cid[c], 0)),
                  pl.BlockSpec((None, D, tf), lambda c, f, cid, ce, ns: (ce[c], 0, fidx(c, f, ns))),
                  pl.BlockSpec((None, D, tf), lambda c, f, cid, ce, ns: (ce[c], 0, fidx(c, f, ns))),
                  pl.BlockSpec((None, tf, D), lambda c, f, cid, ce, ns: (ce[c], fidx(c, f, ns), 0))],
        out_specs=pl.BlockSpec((r, D), lambda c, f, cid, ce, ns: (c, 0)),
        scratch_shapes=[pltpu.VMEM((D, tf), BF16), pltpu.VMEM((D, tf), BF16), pltpu.VMEM((tf, D), BF16)],
    )
    return pl.pallas_call(
        _moe_ffn_kernel,
        grid_spec=grid_spec,
        out_shape=jax.ShapeDtypeStruct((nc * r, D), F32),
        compiler_params=_cp(("arbitrary", "arbitrary"), 56),
        name="moe_ffn",
    )(cid, ce, nsub, hs, wg, wu, wd)


def _combine_kernel(p1_ref, p2_ref, y_hbm, x_ref, gate_ref, wt_ref, o_ref, buf, sem, *, tt):
    i = pl.program_id(0)

    def row_copy(k, r, t):
        return pltpu.make_async_copy(y_hbm.at[pl.ds(t, 1)], buf.at[k, pl.ds(r, 1)], sem)

    def issue(r, c):
        row_copy(0, r, p1_ref[i * tt + r]).start()
        row_copy(1, r, p2_ref[i * tt + r]).start()
        return c

    lax.fori_loop(0, tt, issue, 0)

    def drain(r, c):
        row_copy(0, r, 0).wait()
        row_copy(1, r, 0).wait()
        return c

    lax.fori_loop(0, tt, drain, 0)
    wt = wt_ref[...]
    y = wt[:, 0:1] * buf[0] + wt[:, 1:2] * buf[1]
    o_ref[...] = x_ref[...] + gate_ref[...] * y


def _combine(y, pos1, pos2, x_all, gate_all, wt, tt=128):
    t = x_all.shape[0]
    blk = pl.BlockSpec((tt, D), lambda i, *_: (i, 0))
    grid_spec = pltpu.PrefetchScalarGridSpec(
        num_scalar_prefetch=2,
        grid=(t // tt,),
        in_specs=[pl.BlockSpec(memory_space=pl.ANY), blk, blk, pl.BlockSpec((tt, 128), lambda i, *_: (i, 0))],
        out_specs=blk,
        scratch_shapes=[pltpu.VMEM((2, tt, D), F32), pltpu.SemaphoreType.DMA(())],
    )
    return pl.pallas_call(
        functools.partial(_combine_kernel, tt=tt),
        grid_spec=grid_spec,
        out_shape=jax.ShapeDtypeStruct((t, D), F32),
        compiler_params=_cp(("arbitrary",), 32),
        name="moe_combine",
    )(pos1, pos2, y, x_all, gate_all, wt)


def _moe(xp2d, xs2d, modp, mods, g, wr, wg, wu, wd):
    tp, ts = xp2d.shape[0], xs2d.shape[0]
    wr_t = jnp.pad(wr, ((0, 0), (0, 128 - NE))).astype(BF16)
    hp, ip, wtp = _router(xp2d, g, modp["sc2"], modp["sh2"], modp["rpg"], 1, wr_t, 256)
    hsm, is_, wts = _router(xs2d, g, mods["sc2"], mods["sh2"], ts, ts, wr_t, ts)
    h_all = jnp.concatenate([hp, hsm], axis=0)
    idx = jnp.concatenate([ip, is_], axis=0)
    wt = jnp.concatenate([wtp, wts], axis=0)
    t = tp + ts
    nc = (2 * t) // MOE_CHUNK + NE
    pos, src, cid, ce, nsub = _moe_plan(idx[:, :2], nc)
    hs = _gather_rows(h_all, src, nsub, nc)
    y = _moe_ffn(hs, cid, ce, nsub, wg, wu, wd, nc)
    x_all = jnp.concatenate([xp2d, xs2d], axis=0)
    gate_all = jnp.concatenate([jnp.repeat(modp["g2"][:, 0, :], modp["rpg"], axis=0), mods["g2"][0]], axis=0)
    out = _combine(y, pos[0::2], pos[1::2], x_all, gate_all, wt)
    return out[:tp], out[tp:]


def kernel(x_prompt, x_sample, cache_k, cache_v, cache_logf, page_table, c_prompt, c_sample, norm1_g, norm2_g, w_ada, b_ada, w_in, b_forget, q_norm_g, k_norm_g, sgu_ln_g, sgu_ln_b, sgu_w, sgu_b, w_branch_attn, w_branch_sgu, w_out, ffd_w_gate, ffd_w_up, ffd_w_down, w_router, moe_w_gate, moe_w_up, moe_w_down):
    nb, seq, _ = x_prompt.shape
    db, nd, _ = x_sample.shape
    depth, npool = cache_k.shape[:2]
    tp, ts = nb * seq, db * nd

    c_all = jnp.concatenate([c_prompt, c_sample], axis=0)
    mc = -(-c_all.shape[0] // 8) * 8
    mod_all = _ada(jnp.pad(c_all, ((0, mc - c_all.shape[0]), (0, 0))), w_ada, b_ada)

    cache_k2 = cache_k.reshape(depth, npool, CH * NH, DH)
    cache_v2 = cache_v.reshape(depth, npool, CH * NH, DH)
    rloc, tot = _logf_pages(cache_logf.reshape(depth, npool, CH * NH))

    eye_seq = jnp.eye(db, dtype=F32)
    xp, xs = x_prompt.reshape(tp, D), x_sample.reshape(ts, D)
    outs = {k: [] for k in ("kp", "vp", "fp", "ks", "vs", "fs", "ss")}
    for l in range(depth):
        names = ("sh1", "sc1", "g1", "sh2", "sc2", "g2")
        modp = {n: mod_all[l, :nb, i * D:(i + 1) * D][:, None, :] for i, n in enumerate(names)}
        mods = {n: jnp.repeat(mod_all[l, nb:nb + db, i * D:(i + 1) * D], nd, axis=0)[None] for i, n in enumerate(names)}
        modp["rpg"] = seq

        wl = w_in[l]
        cut = 3 * AW
        wp = jnp.concatenate([wl[:, :cut], wl[:, cut + NH:]], axis=1).astype(BF16)
        wf = jnp.pad(wl[:, cut:cut + NH], ((0, 0), (0, 128 - NH))).astype(BF16)
        bf = jnp.pad(b_forget[l], (0, 128 - NH))[None]
        par = (wp, wf, bf, q_norm_g[l][None], k_norm_g[l][None], sgu_ln_g[l][None], sgu_ln_b[l][None])
        n1 = norm1_g[l][None]
        n2 = norm2_g[l][None]

        q, k, v, lf, u, vs, ga, gb = _inproj(xp, n1, modp["sc1"], modp["sh1"], seq, 1, *par, tm=512)
        lf = lf[:, :NH]
        crow = _cumsum(lf.reshape(nb, seq, NH).transpose(0, 2, 1))
        o = _flash(q.reshape(nb, seq, AW), k.reshape(nb, seq, AW), v.reshape(nb, seq, AW),
                   crow.transpose(0, 2, 1), crow)
        bmat = jnp.broadcast_to(sgu_b[l][:, :, None], (NG, CH, CH))
        sg = _sgu(u, vs, sgu_w[l], bmat, 512)
        mg = _merge(o.reshape(tp, AW), sg, w_branch_attn[l], w_branch_sgu[l], ga, gb, 1024)
        xp = _outproj(mg, w_out[l], xp, modp["g1"], seq, 1, 1024)
        outs["kp"].append(k.reshape(nb, seq, NH, DH))
        outs["vp"].append(v.reshape(nb, seq, NH, DH))
        outs["fp"].append(lf.reshape(nb, seq, NH))

        q, k, v, lf, u, vs, ga, gb = _inproj(xs, n1, mods["sc1"], mods["sh1"], ts, ts, *par, tm=ts)
        lf = lf[:, :NH]
        q_hq = q.reshape(db, nd, NH, DH).transpose(0, 2, 1, 3).reshape(db, NH * nd, DH)
        o = _paged(l, page_table, q_hq, k.reshape(db, nd * NH, DH), v.reshape(db, nd * NH, DH),
                   lf.reshape(db, 1, nd * NH), cache_k2, cache_v2, rloc, tot)
        o = o.reshape(db, NH, nd, DH).transpose(0, 2, 1, 3).reshape(ts, AW)
        w4 = sgu_w[l][:, :nd, :nd]
        wmix = (eye_seq[None, :, None, :, None] * w4[:, None, :, None, :]).reshape(NG, ts, ts)
        bmat = jnp.broadcast_to(jnp.tile(sgu_b[l][:, :nd], (1, db))[:, :, None], (NG, CH, CH))
        sg = _sgu(u, vs, wmix, bmat, ts)
        mg = _merge(o, sg, w_branch_attn[l], w_branch_sgu[l], ga, gb, ts)
        xs = _outproj(mg, w_out[l], xs, mods["g1"], ts, ts, ts)
        outs["ks"].append(k.reshape(db, nd, NH, DH))
        outs["vs"].append(v.reshape(db, nd, NH, DH))
        outs["fs"].append(lf.reshape(db, nd, NH))
        outs["ss"].append(vs.reshape(db, nd, SW))

        j = l // 2
        if l % 2 == 0:
            wts = (ffd_w_gate[j], ffd_w_up[j], ffd_w_down[j])
            xp = _ffn_dense(xp, n2, modp["sc2"], modp["sh2"], modp["g2"], seq, 1, *wts, tm=512)
            xs = _ffn_dense(xs, n2, mods["sc2"], mods["sh2"], mods["g2"], ts, ts, *wts, tm=ts)
        else:
            xp, xs = _moe(xp, xs, modp, mods, n2, w_router[j], moe_w_gate[j], moe_w_up[j], moe_w_down[j])

    st = lambda n: jnp.stack(outs[n])
    return (xp.reshape(nb, seq, D), xs.reshape(db, nd, D), st("kp"), st("vp"), st("fp"),
            st("ks"), st("vs"), st("fs"), st("ss"))
```

```python
import functools
import math

import jax
import jax.numpy as jnp
from jax import lax
from jax.experimental import pallas as pl
from jax.experimental.pallas import tpu as pltpu

F32, BF16, I32 = jnp.float32, jnp.bfloat16, jnp.int32

D = 2048
NH, DH = 8, 128
AW = NH * DH
SW = 1024
NG, GW = 8, 128
CH = 128
NE = 8
EPS = 1e-6
NEG = -1e30
LOG2E = math.log2(math.e)

MOE_CHUNK = 1280
MOE_SUB = 256
PAGES_PER_STEP = 8
FFN_TF = 512


def _cp(sem, mb):
    return pltpu.CompilerParams(dimension_semantics=sem, vmem_limit_bytes=mb << 20)


def _normmod(x, g, sc, sh):
    y = x * lax.rsqrt(jnp.mean(x * x, axis=-1, keepdims=True) + EPS)
    return (y * g) * (1.0 + sc) + sh


def _split3(x):
    x1 = x.astype(BF16)
    r1 = x - x1.astype(F32)
    x2 = r1.astype(BF16)
    r2 = r1 - x2.astype(F32)
    return x1, x2, r2.astype(BF16)


def _dot3(x, m):
    return sum(jnp.dot(p, m, preferred_element_type=F32) for p in _split3(x))


def _dot_nt(a, b):
    return lax.dot_general(a, b, (((1,), (1,)), ((), ())), preferred_element_type=F32)


def _silu(x):
    return x * jax.nn.sigmoid(x)


def _mod_spec(tm, rows_per_group, r):
    return pl.BlockSpec((None, r, D), lambda i, *_: ((i * tm) // rows_per_group, 0, 0))


def _ada_kernel(c_ref, w_ref, b_ref, o_ref):
    a = _silu(c_ref[...]).astype(BF16)
    o_ref[...] = jnp.dot(a, w_ref[...].astype(BF16), preferred_element_type=F32) + b_ref[...]


def _ada(c_all, w_ada, b_ada):
    nl, _, n = w_ada.shape
    mc = c_all.shape[0]
    tn = 1024
    return pl.pallas_call(
        _ada_kernel,
        grid=(nl, n // tn),
        in_specs=[pl.BlockSpec((mc, D), lambda l, j: (0, 0)),
                  pl.BlockSpec((None, D, tn), lambda l, j: (l, 0, j)),
                  pl.BlockSpec((None, 1, tn), lambda l, j: (l, 0, j))],
        out_specs=pl.BlockSpec((None, mc, tn), lambda l, j: (l, 0, j)),
        out_shape=jax.ShapeDtypeStruct((nl, mc, n), F32),
        compiler_params=_cp(("parallel", "parallel"), 40),
        name="ada",
    )(c_all, w_ada, b_ada.reshape(nl, 1, n))


_TN_IN = 512
_J_Q, _J_K, _J_V, _J_R, _J_END = 0, 2, 4, 6, 18
_R_U, _R_VS, _R_GA, _R_GB = 0, SW, 2 * SW, 2 * SW + D


def _headnorm(z, g):
    outs = []
    for hh in range(z.shape[1] // DH):
        zz = z[:, hh * DH:(hh + 1) * DH]
        outs.append(zz * lax.rsqrt(jnp.mean(zz * zz, axis=-1, keepdims=True) + EPS) * g)
    return jnp.concatenate(outs, axis=-1)


def _inproj_kernel(x_ref, g_ref, sc_ref, sh_ref, w_ref, wf_ref, bf_ref, qg_ref, kg_ref,
                   q_ref, k_ref, v_ref, lf_ref, r_ref, h_scr, *, qscale):
    j = pl.program_id(1)

    @pl.when(j == 0)
    def _():
        h = _normmod(x_ref[...], g_ref[...], sc_ref[...], sh_ref[...]).astype(BF16)
        h_scr[...] = h
        f = jnp.dot(h, wf_ref[...], preferred_element_type=F32) + bf_ref[...]
        lf_ref[...] = jnp.minimum(f, 0.0) - jnp.log1p(jnp.exp(-jnp.abs(f)))

    z = jnp.dot(h_scr[...], w_ref[...], preferred_element_type=F32)
    r_ref[...] = z

    @pl.when(j < _J_K)
    def _():
        q_ref[...] = (_headnorm(z, qg_ref[...]) * qscale).astype(BF16)

    @pl.when((j >= _J_K) & (j < _J_V))
    def _():
        k_ref[...] = _headnorm(z, kg_ref[...])

    @pl.when((j >= _J_V) & (j < _J_R))
    def _():
        v_ref[...] = z


def _inproj(layer, x2d, g, sc, sh, rpg, r, wp, wf, bf, qg, kg, tm, qscale):
    t = x2d.shape[0]
    tn = _TN_IN

    def cspec(j0, nb):
        return pl.BlockSpec((tm, tn), lambda i, j: (i, jnp.clip(j - j0, 0, nb - 1)))

    full = lambda a: pl.BlockSpec(a.shape, lambda i, j: (0,) * a.ndim)
    return pl.pallas_call(
        functools.partial(_inproj_kernel, qscale=qscale),
        grid=(t // tm, _J_END),
        in_specs=[pl.BlockSpec((tm, D), lambda i, j: (i, 0)), full(g), _mod_spec(tm, rpg, r), _mod_spec(tm, rpg, r),
                  pl.BlockSpec((None, D, tn), lambda i, j: (layer, 0, j)),
                  pl.BlockSpec((None, D, 128), lambda i, j: (layer, 0, 0)),
                  full(bf), full(qg), full(kg)],
        out_specs=[cspec(_J_Q, 2), cspec(_J_K, 2), cspec(_J_V, 2),
                   pl.BlockSpec((tm, 128), lambda i, j: (i, 0)),
                   cspec(_J_R, _J_END - _J_R)],
        out_shape=[jax.ShapeDtypeStruct((t, AW), BF16), jax.ShapeDtypeStruct((t, AW), F32),
                   jax.ShapeDtypeStruct((t, AW), F32), jax.ShapeDtypeStruct((t, 128), F32),
                   jax.ShapeDtypeStruct((t, 2 * SW + 2 * D), F32)],
        scratch_shapes=[pltpu.VMEM((tm, D), BF16)],
        compiler_params=_cp(("parallel", "arbitrary"), 52),
        name="inproj",
    )(x2d, g, sc, sh, wp, wf, bf, qg, kg)


def _cumsum_kernel(x_ref, o_ref, *, scale):
    s = x_ref.shape[-1]
    r = lax.broadcasted_iota(I32, (128, 128), 0)
    c = lax.broadcasted_iota(I32, (128, 128), 1)
    tri = jnp.where(r <= c, 1.0, 0.0).astype(BF16)
    carry = jnp.zeros((NH, 1), F32)
    for b in range(s // 128):
        y = _dot3(x_ref[:, b * 128:(b + 1) * 128], tri) + carry
        o_ref[:, b * 128:(b + 1) * 128] = y * scale
        carry = y[:, 127:128]


def _cumsum(lft, scale):
    b, _, s = lft.shape
    return pl.pallas_call(
        functools.partial(_cumsum_kernel, scale=scale),
        grid=(b,),
        in_specs=[pl.BlockSpec((None, NH, s), lambda i: (i, 0, 0))],
        out_specs=pl.BlockSpec((None, NH, s), lambda i: (i, 0, 0)),
        out_shape=jax.ShapeDtypeStruct((b, NH, s), F32),
        compiler_params=_cp(("parallel",), 32),
        name="logf_cumsum",
    )(lft)


def _flash_kernel(q_ref, k_ref, v_ref, cc_ref, cr_ref, o_ref, *, t):
    h = pl.program_id(1)
    i = pl.program_id(2)
    q = q_ref[...]
    lane = lax.broadcasted_iota(I32, (t, NH), 1)
    cq = jnp.sum(jnp.where(lane == h, cc_ref[...], 0.0), axis=-1, keepdims=True)

    def step(kb, carry, masked):
        m, l, acc = carry
        k0 = pl.multiple_of(kb * t, t)
        k = k_ref[pl.ds(k0, t), :].astype(BF16)
        v = v_ref[pl.ds(k0, t), :].astype(BF16)
        ck = cr_ref[pl.ds(h, 1), pl.ds(k0, t)]
        s = _dot_nt(q, k) + cq - ck
        if masked:
            row = lax.broadcasted_iota(I32, (t, t), 0)
            col = lax.broadcasted_iota(I32, (t, t), 1)
            s = jnp.where(col <= row, s, NEG)
        m_new = jnp.maximum(m, jnp.max(s, axis=-1, keepdims=True))
        alpha = jnp.exp2(m - m_new)
        p = jnp.exp2(s - m_new)
        l = alpha * l + jnp.sum(p, axis=-1, keepdims=True)
        acc = alpha * acc + jnp.dot(p.astype(BF16), v, preferred_element_type=F32)
        return m_new, l, acc

    init = (jnp.full((t, 1), NEG, F32), jnp.zeros((t, 1), F32), jnp.zeros((t, DH), F32))
    carry = lax.fori_loop(0, i, lambda kb, c: step(kb, c, False), init)
    _, l, acc = step(i, carry, True)
    o_ref[...] = (acc / l).astype(BF16)


def _flash(q, k, v, ccol, crow, t=512):
    b, s, _ = q.shape
    return pl.pallas_call(
        functools.partial(_flash_kernel, t=t),
        grid=(b, NH, s // t),
        in_specs=[pl.BlockSpec((None, t, DH), lambda bb, h, i: (bb, i, h)),
                  pl.BlockSpec((None, s, DH), lambda bb, h, i: (bb, 0, h)),
                  pl.BlockSpec((None, s, DH), lambda bb, h, i: (bb, 0, h)),
                  pl.BlockSpec((None, t, NH), lambda bb, h, i: (bb, i, 0)),
                  pl.BlockSpec((None, NH, s), lambda bb, h, i: (bb, 0, 0))],
        out_specs=pl.BlockSpec((None, t, DH), lambda bb, h, i: (bb, i, h)),
        out_shape=jax.ShapeDtypeStruct((b, s, AW), BF16),
        compiler_params=_cp(("parallel", "parallel", "arbitrary"), 40),
        name="fox_prompt",
    )(q, k, v, ccol, crow)


def _logf_pages_kernel(x_ref, ms_ref, mt_ref, r_ref, t_ref):
    p1, p2, p3 = _split3(x_ref[...])
    ms, mt = ms_ref[...], mt_ref[...]
    dot = lambda a, m: jnp.dot(a, m, preferred_element_type=F32)
    r_ref[...] = dot(p1, ms) + dot(p2, ms) + dot(p3, ms)
    t_ref[...] = dot(p1, mt) + dot(p2, mt) + dot(p3, mt)


def _logf_pages(lf_flat):
    nl, npool, w = lf_flat.shape
    tp = math.gcd(npool, 256)
    a = jnp.arange(w)
    same_head = (a[:, None] % NH) == (a[None, :] % NH)
    ms = (same_head & (a[:, None] // NH > a[None, :] // NH)).astype(BF16)
    mt = same_head.astype(BF16)
    cm = pl.BlockSpec((w, w), lambda l, i: (0, 0))
    blk = pl.BlockSpec((None, tp, w), lambda l, i: (l, i, 0))
    return pl.pallas_call(
        _logf_pages_kernel,
        grid=(nl, npool // tp),
        in_specs=[blk, cm, cm],
        out_specs=[blk, blk],
        out_shape=[jax.ShapeDtypeStruct((nl, npool, w), F32)] * 2,
        compiler_params=_cp(("parallel", "parallel"), 32),
        name="logf_pages",
    )(lf_flat, ms, mt)


def _paged_kernel(pt_ref, q_ref, kn_ref, vn_ref, lfn_ref, *refs, g, npages):
    ks, vs, rl, tt = refs[:g], refs[g:2 * g], refs[2 * g:3 * g], refs[3 * g:4 * g]
    o_ref = refs[4 * g]
    m_scr, l_scr, acc_scr, carry_scr, cn_scr = refs[4 * g + 1:]
    b = pl.program_id(0)
    s = pl.program_id(1)
    ns = pl.num_programs(1)
    nq = 4 * NH
    w = CH * NH
    q = q_ref[...]

    @pl.when(s == 0)
    def _():
        r = lax.broadcasted_iota(I32, (nq, nq), 0)
        c = lax.broadcasted_iota(I32, (nq, nq), 1)
        lf = lfn_ref[...]
        m_kh = jnp.where((r % NH == c % NH) & (r // NH <= c // NH), 1.0, 0.0).astype(BF16)
        m_hq = jnp.where((r % NH == c // 4) & (r // NH <= c % 4), 1.0, 0.0).astype(BF16)
        cn_kh = _dot3(lf, m_kh)
        cn_hq = _dot3(lf, m_hq)
        cn_col = jnp.sum(jnp.where(r == c, jnp.broadcast_to(cn_hq, (nq, nq)), 0.0), axis=1, keepdims=True)
        sn = _dot_nt(q, kn_ref[...].astype(BF16)) + cn_col - cn_kh
        sn = jnp.where((r // 4 == c % NH) & (c // NH <= r % 4), sn, NEG)
        m = jnp.max(sn, axis=-1, keepdims=True)
        p = jnp.exp(sn - m)
        m_scr[...] = jnp.broadcast_to(m, m_scr.shape)
        l_scr[...] = jnp.broadcast_to(jnp.sum(p, axis=-1, keepdims=True), l_scr.shape)
        acc_scr[...] = jnp.dot(p.astype(BF16), vn_ref[...].astype(BF16), preferred_element_type=F32)
        cn_scr[...] = jnp.broadcast_to(cn_col, cn_scr.shape)
        carry_scr[...] = jnp.zeros_like(carry_scr)

    row = lax.broadcasted_iota(I32, (nq, w), 0)
    col = lax.broadcasted_iota(I32, (nq, w), 1)
    head_ok = (row // 4) == (col % NH)
    cn_col = cn_scr[:, 0:1]
    m, l, carry = m_scr[:, 0:1], l_scr[:, 0:1], carry_scr[...]
    sts = []
    for gi in reversed(range(g)):
        page = pt_ref[b * npages + (ns - 1 - s) * g + gi]
        sub = page % 8
        bias = rl[gi][pl.ds(sub, 1), :] + carry
        carry = carry + tt[gi][pl.ds(sub, 1), :]
        st = _dot_nt(q, ks[gi][...].astype(BF16)) + bias + cn_col
        sts.append((gi, jnp.where(head_ok, st, NEG)))
    m_new = jnp.maximum(m, jnp.max(functools.reduce(jnp.maximum, [st for _, st in sts]), axis=-1, keepdims=True))
    alpha = jnp.exp(m - m_new)
    ps = [(gi, jnp.exp(st - m_new)) for gi, st in sts]
    l = alpha * l + jnp.sum(functools.reduce(jnp.add, [p for _, p in ps]), axis=-1, keepdims=True)
    pv = [jnp.dot(p.astype(BF16), vs[gi][...].astype(BF16), preferred_element_type=F32) for gi, p in ps]
    acc = alpha * acc_scr[...] + functools.reduce(jnp.add, pv)
    m_scr[...] = jnp.broadcast_to(m_new, m_scr.shape)
    l_scr[...] = jnp.broadcast_to(l, l_scr.shape)
    acc_scr[...] = acc
    carry_scr[...] = carry

    @pl.when(s == ns - 1)
    def _():
        o_ref[...] = (acc / l).astype(BF16)


def _paged(layer, page_table, q_hq, kn, vn, lfn, cache_k2, cache_v2, rloc, tot):
    db, nq, _ = q_hq.shape
    npages = page_table.shape[1]
    g = PAGES_PER_STEP
    ns = npages // g
    w = CH * NH

    def page_of(b, s, pt, gi):
        return pt[b * npages + (ns - 1 - s) * g + gi]

    per_seq = lambda shp: pl.BlockSpec((None,) + shp, lambda b, s, pt: (b, 0, 0))
    kv_specs = [pl.BlockSpec((None, None, w, DH), lambda b, s, pt, gi=gi: (layer, page_of(b, s, pt, gi), 0, 0))
                for gi in range(g)]
    lf_specs = [pl.BlockSpec((None, 8, w), lambda b, s, pt, gi=gi: (layer, page_of(b, s, pt, gi) // 8, 0))
                for gi in range(g)]
    grid_spec = pltpu.PrefetchScalarGridSpec(
        num_scalar_prefetch=1,
        grid=(db, ns),
        in_specs=[per_seq((nq, DH)), per_seq((nq, DH)), per_seq((nq, DH)), per_seq((1, nq))]
        + kv_specs + kv_specs + lf_specs + lf_specs,
        out_specs=per_seq((nq, DH)),
        scratch_shapes=[pltpu.VMEM((nq, 128), F32), pltpu.VMEM((nq, 128), F32), pltpu.VMEM((nq, DH), F32),
                        pltpu.VMEM((1, w), F32), pltpu.VMEM((nq, 128), F32)],
    )
    return pl.pallas_call(
        functools.partial(_paged_kernel, g=g, npages=npages),
        grid_spec=grid_spec,
        out_shape=jax.ShapeDtypeStruct((db, nq, DH), BF16),
        compiler_params=_cp(("parallel", "arbitrary"), 48),
        name="fox_decode",
    )(page_table.reshape(-1), q_hq, kn, vn, lfn, *([cache_k2] * g), *([cache_v2] * g), *([rloc] * g), *([tot] * g))


def _mix_kernel(oa_ref, u_ref, vs_ref, ga_ref, gb_ref, w_ref, b_ref, lng_ref, lnb_ref, wa_ref, ws_ref,
                o_ref, vsn_ref, sg_scr):
    j = pl.program_id(1)
    tm = u_ref.shape[0]

    @pl.when(j == 0)
    def _():
        t = jax.nn.gelu(vs_ref[...])
        mu = jnp.mean(t, axis=-1, keepdims=True)
        tc = t - mu
        var = jnp.mean(tc * tc, axis=-1, keepdims=True)
        vsn_ref[...] = tc * lax.rsqrt(var + EPS) * lng_ref[...] + lnb_ref[...]
        r = lax.broadcasted_iota(I32, (CH, CH), 0)
        c = lax.broadcasted_iota(I32, (CH, CH), 1)
        for gi in range(NG):
            wg = jnp.where(c <= r, w_ref[gi], 0.0).astype(BF16)
            bg = b_ref[gi]
            cs = slice(gi * GW, (gi + 1) * GW)
            for ci in range(tm // CH):
                rs = slice(ci * CH, (ci + 1) * CH)
                mixed = jnp.dot(wg, vsn_ref[rs, cs].astype(BF16), preferred_element_type=F32) + bg
                sg_scr[rs, cs] = (jax.nn.gelu(u_ref[rs, cs]) * mixed).astype(BF16)

    a = jnp.dot(oa_ref[...], wa_ref[...], preferred_element_type=F32)
    s = jnp.dot(sg_scr[...], ws_ref[...], preferred_element_type=F32)
    o_ref[...] = (jax.nn.sigmoid(ga_ref[...]) * a + jax.nn.sigmoid(gb_ref[...]) * s).astype(BF16)


def _mix(layer, oa, rest, wmix, bmat, lng, lnb, wa, ws, tm, tn=512):
    t = oa.shape[0]
    row = lambda off: pl.BlockSpec((tm, SW), lambda i, j: (i, off // SW))
    gate = lambda off: pl.BlockSpec((tm, tn), lambda i, j: (i, off // tn + j))
    par = pl.BlockSpec((NG, CH, CH), lambda i, j: (0, 0, 0))
    vec = pl.BlockSpec((1, SW), lambda i, j: (0, 0))
    w_spec = pl.BlockSpec((None, AW, tn), lambda i, j: (layer, 0, j))
    return pl.pallas_call(
        _mix_kernel,
        grid=(t // tm, D // tn),
        in_specs=[pl.BlockSpec((tm, AW), lambda i, j: (i, 0)), row(_R_U), row(_R_VS), gate(_R_GA), gate(_R_GB),
                  par, par, vec, vec, w_spec, w_spec],
        out_specs=[pl.BlockSpec((tm, tn), lambda i, j: (i, j)), pl.BlockSpec((tm, SW), lambda i, j: (i, 0))],
        out_shape=[jax.ShapeDtypeStruct((t, D), BF16), jax.ShapeDtypeStruct((t, SW), F32)],
        scratch_shapes=[pltpu.VMEM((tm, SW), BF16)],
        compiler_params=_cp(("parallel", "arbitrary"), 48),
        name="mix",
    )(oa, rest, rest, rest, rest, wmix, bmat, lng, lnb, wa, ws)


def _outproj_kernel(a_ref, w_ref, x_ref, g_ref, o_ref):
    y = jnp.dot(a_ref[...], w_ref[...], preferred_element_type=F32)
    o_ref[...] = x_ref[...] + g_ref[...] * y


def _outproj(layer, a, w, x2d, gate, rpg, r, tm, tn=512):
    t = a.shape[0]
    return pl.pallas_call(
        _outproj_kernel,
        grid=(t // tm, D // tn),
        in_specs=[pl.BlockSpec((tm, D), lambda i, j: (i, 0)),
                  pl.BlockSpec((None, D, tn), lambda i, j: (layer, 0, j)),
                  pl.BlockSpec((tm, tn), lambda i, j: (i, j)),
                  pl.BlockSpec((None, r, tn), lambda i, j: ((i * tm) // rpg, 0, j))],
        out_specs=pl.BlockSpec((tm, tn), lambda i, j: (i, j)),
        out_shape=jax.ShapeDtypeStruct((t, D), F32),
        compiler_params=_cp(("parallel", "parallel"), 48),
        name="outproj",
    )(a, w, x2d, gate)


def _ffn_kernel(x_ref, g_ref, sc_ref, sh_ref, gate_ref, wg_ref, wu_ref, wd_ref, o_ref, h_scr):
    f = pl.program_id(1)

    @pl.when(f == 0)
    def _():
        h_scr[...] = _normmod(x_ref[...], g_ref[...], sc_ref[...], sh_ref[...]).astype(BF16)
        o_ref[...] = jnp.zeros_like(o_ref)

    h = h_scr[...]
    gg = jnp.dot(h, wg_ref[...], preferred_element_type=F32)
    uu = jnp.dot(h, wu_ref[...], preferred_element_type=F32)
    a = (_silu(gg) * uu).astype(BF16)
    o_ref[...] += jnp.dot(a, wd_ref[...], preferred_element_type=F32)

    @pl.when(f == pl.num_programs(1) - 1)
    def _():
        o_ref[...] = x_ref[...] + gate_ref[...] * o_ref[...]


def _ffn_dense(layer, x2d, g, sc, sh, gate, rpg, r, wg, wu, wd, tm, vmem_mb):
    t = x2d.shape[0]
    tf = FFN_TF
    full = lambda a: pl.BlockSpec(a.shape, lambda i, j: (0,) * a.ndim)
    ms = _mod_spec(tm, rpg, r)
    return pl.pallas_call(
        _ffn_kernel,
        grid=(t // tm, wg.shape[2] // tf),
        in_specs=[pl.BlockSpec((tm, D), lambda i, j: (i, 0)), full(g), ms, ms, ms,
                  pl.BlockSpec((None, D, tf), lambda i, j: (layer, 0, j)),
                  pl.BlockSpec((None, D, tf), lambda i, j: (layer, 0, j)),
                  pl.BlockSpec((None, tf, D), lambda i, j: (layer, j, 0))],
        out_specs=pl.BlockSpec((tm, D), lambda i, j: (i, 0)),
        out_shape=jax.ShapeDtypeStruct((t, D), F32),
        scratch_shapes=[pltpu.VMEM((tm, D), BF16)],
        compiler_params=_cp(("parallel", "arbitrary"), vmem_mb),
        name="ffn_dense",
    )(x2d, g, sc, sh, gate, wg, wu, wd)


def _router_kernel(x_ref, g_ref, sc_ref, sh_ref, wr_ref, h_ref, idx_ref, wt_ref):
    h = _normmod(x_ref[...], g_ref[...], sc_ref[...], sh_ref[...])
    h_ref[...] = h
    logits = jnp.dot(h.astype(BF16), wr_ref[...], preferred_element_type=F32)
    ls = [logits[:, e:e + 1] for e in range(NE)]
    m1 = functools.reduce(jnp.maximum, ls)
    i1 = jnp.full(m1.shape, NE, I32)
    for e in reversed(range(NE)):
        i1 = jnp.where(ls[e] == m1, e, i1)
    ls2 = [jnp.where(i1 == e, -jnp.inf, ls[e]) for e in range(NE)]
    m2 = functools.reduce(jnp.maximum, ls2)
    i2 = jnp.full(m1.shape, NE, I32)
    for e in reversed(range(NE)):
        i2 = jnp.where(ls2[e] == m2, e, i2)
    e2 = jnp.exp(m2 - m1)
    p1 = 1.0 / (1.0 + e2)
    lane = lax.broadcasted_iota(I32, idx_ref.shape, 1)
    idx_ref[...] = jnp.where(lane == 0, i1, jnp.where(lane == 1, i2, 0))
    wt_ref[...] = jnp.where(lane == 0, p1, jnp.where(lane == 1, e2 * p1, 0.0))


def _router(x2d, g, sc, sh, rpg, r, wr_pad, tm):
    t = x2d.shape[0]
    full = lambda a: pl.BlockSpec(a.shape, lambda i: (0,) * a.ndim)
    ms = _mod_spec(tm, rpg, r)
    return pl.pallas_call(
        _router_kernel,
        grid=(t // tm,),
        in_specs=[pl.BlockSpec((tm, D), lambda i: (i, 0)), full(g), ms, ms, full(wr_pad)],
        out_specs=[pl.BlockSpec((tm, D), lambda i: (i, 0)), pl.BlockSpec((tm, 128), lambda i: (i, 0)),
                   pl.BlockSpec((tm, 128), lambda i: (i, 0))],
        out_shape=[jax.ShapeDtypeStruct((t, D), F32), jax.ShapeDtypeStruct((t, 128), I32),
                   jax.ShapeDtypeStruct((t, 128), F32)],
        compiler_params=_cp(("parallel",), 32),
        name="router",
    )(x2d, g, sc, sh, wr_pad)


def _moe_plan(idx2, nc):
    r = MOE_CHUNK
    e_flat = idx2.reshape(-1)
    npairs = e_flat.shape[0]
    onehot = (e_flat[:, None] == jnp.arange(NE, dtype=I32)[None, :]).astype(I32)
    csum = jnp.cumsum(onehot, axis=0)
    rank = jnp.sum(onehot * csum, axis=1) - 1
    counts = csum[-1]
    nch = (counts + r - 1) // r
    cstart = jnp.cumsum(nch) - nch
    n_used = jnp.sum(nch)
    chunk = jnp.sum(onehot * cstart[None, :], axis=1) + rank // r
    pos = chunk * r + rank % r
    src = jnp.zeros((nc * r,), I32).at[pos].set(jnp.arange(npairs, dtype=I32) // 2)
    cid = jnp.minimum(jnp.arange(nc, dtype=I32), n_used - 1)
    ce = jnp.sum((cid[:, None] >= cstart[None, :]).astype(I32), axis=1) - 1
    rows = jnp.clip(counts[ce] - (cid - cstart[ce]) * r, 0, r)
    rows = jnp.where(jnp.arange(nc) < n_used, rows, 0)
    nsub = (rows + MOE_SUB - 1) // MOE_SUB
    return pos, src, cid, ce, nsub


def _gather_kernel(src_ref, nsub_ref, h_hbm, o_ref, buf, sem):
    s = pl.program_id(0)
    per_chunk = MOE_CHUNK // MOE_SUB
    live = (s % per_chunk) < nsub_ref[s // per_chunk]

    def row_copy(r, t):
        return pltpu.make_async_copy(h_hbm.at[pl.ds(t, 1)], buf.at[pl.ds(r, 1)], sem)

    @pl.when(live)
    def _():
        def issue(r, c):
            row_copy(r, src_ref[s * MOE_SUB + r]).start()
            return c

        lax.fori_loop(0, MOE_SUB, issue, 0, unroll=8)

        def drain(r, c):
            row_copy(r, 0).wait()
            return c

        lax.fori_loop(0, MOE_SUB, drain, 0, unroll=8)
        o_ref[...] = buf[...].astype(BF16)

    @pl.when(jnp.logical_not(live))
    def _():
        o_ref[...] = jnp.zeros_like(o_ref)


def _gather_rows(h, src, nsub, nc):
    nrows = nc * MOE_CHUNK
    grid_spec = pltpu.PrefetchScalarGridSpec(
        num_scalar_prefetch=2,
        grid=(nrows // MOE_SUB,),
        in_specs=[pl.BlockSpec(memory_space=pl.ANY)],
        out_specs=pl.BlockSpec((MOE_SUB, D), lambda s, *_: (s, 0)),
        scratch_shapes=[pltpu.VMEM((MOE_SUB, D), F32), pltpu.SemaphoreType.DMA(())],
    )
    return pl.pallas_call(
        _gather_kernel,
        grid_spec=grid_spec,
        out_shape=jax.ShapeDtypeStruct((nrows, D), BF16),
        compiler_params=_cp(("arbitrary",), 32),
        name="moe_gather",
    )(src, nsub, h)


def _moe_ffn_kernel(cid_ref, ce_ref, nsub_ref, hs_ref, wg_ref, wu_ref, wd_ref, y_ref, wgb, wub, wdb):
    c = pl.program_id(0)
    f = pl.program_id(1)
    nsub = nsub_ref[c]

    @pl.when(f == 0)
    def _():
        y_ref[...] = jnp.zeros_like(y_ref)

    @pl.when(nsub > 0)
    def _():
        wgb[...] = wg_ref[...].astype(BF16)
        wub[...] = wu_ref[...].astype(BF16)
        wdb[...] = wd_ref[...].astype(BF16)

        def body(s, carry):
            r0 = pl.multiple_of(s * MOE_SUB, MOE_SUB)
            h = hs_ref[pl.ds(r0, MOE_SUB), :]
            gg = jnp.dot(h, wgb[...], preferred_element_type=F32)
            uu = jnp.dot(h, wub[...], preferred_element_type=F32)
            a = (_silu(gg) * uu).astype(BF16)
            y_ref[pl.ds(r0, MOE_SUB), :] += jnp.dot(a, wdb[...], preferred_element_type=F32)
            return carry

        lax.fori_loop(0, nsub, body, 0)


def _moe_ffn(layer, hs, cid, ce, nsub, wg, wu, wd, nc, tf=256):
    dffe = wg.shape[3]
    nf = dffe // tf
    r = MOE_CHUNK

    def fidx(c, f, nsub):
        return jnp.where(nsub[c] > 0, f, nf - 1)

    grid_spec = pltpu.PrefetchScalarGridSpec(
        num_scalar_prefetch=3,
        grid=(nc, nf),
        in_specs=[pl.BlockSpec((r, D), lambda c, f, cid, ce, ns: (cid[c], 0)),
                  pl.BlockSpec((None, None, D, tf), lambda c, f, cid, ce, ns: (layer, ce[c], 0, fidx(c, f, ns))),
                  pl.BlockSpec((None, None, D, tf), lambda c, f, cid, ce, ns: (layer, ce[c], 0, fidx(c, f, ns))),
                  pl.BlockSpec((None, None, tf, D), lambda c, f, cid, ce, ns: (layer, ce[c], fidx(c, f, ns), 0))],
        out_specs=pl.BlockSpec((r, D), lambda c, f, cid, ce, ns: (c, 0)),
        scratch_shapes=[pltpu.VMEM((D, tf), BF16), pltpu.VMEM((D, tf), BF16), pltpu.VMEM((tf, D), BF16)],
    )
    return pl.pallas_call(
        _moe_ffn_kernel,
        grid_spec=grid_spec,
        out_shape=jax.ShapeDtypeStruct((nc * r, D), F32),
        compiler_params=_cp(("arbitrary", "arbitrary"), 56),
        name="moe_ffn",
    )(cid, ce, nsub, hs, wg, wu, wd)


def _combine_kernel(p1_ref, p2_ref, y_hbm, xp_ref, xs_ref, gp_ref, gs_ref, wt_ref, op_ref, os_ref, buf, sem,
                    *, tt, np_tiles):
    i = pl.program_id(0)

    def row_copy(k, r, t):
        return pltpu.make_async_copy(y_hbm.at[pl.ds(t, 1)], buf.at[k, pl.ds(r, 1)], sem)

    def issue(r, c):
        row_copy(0, r, p1_ref[i * tt + r]).start()
        row_copy(1, r, p2_ref[i * tt + r]).start()
        return c

    lax.fori_loop(0, tt, issue, 0, unroll=8)

    def drain(r, c):
        row_copy(0, r, 0).wait()
        row_copy(1, r, 0).wait()
        return c

    lax.fori_loop(0, tt, drain, 0, unroll=8)
    wt = wt_ref[...]
    y = wt[:, 0:1] * buf[0] + wt[:, 1:2] * buf[1]

    @pl.when(i < np_tiles)
    def _():
        op_ref[...] = xp_ref[...] + gp_ref[...] * y

    @pl.when(i >= np_tiles)
    def _():
        os_ref[...] = xs_ref[...] + gs_ref[...] * y


def _combine(y, pos1, pos2, xp2d, xs2d, gate_p, gate_s, rpg, wt):
    tp, ts = xp2d.shape[0], xs2d.shape[0]
    tt = ts
    np_tiles = tp // tt
    pi = lambda i: jnp.minimum(i, np_tiles - 1)
    grid_spec = pltpu.PrefetchScalarGridSpec(
        num_scalar_prefetch=2,
        grid=(np_tiles + 1,),
        in_specs=[pl.BlockSpec(memory_space=pl.ANY),
                  pl.BlockSpec((tt, D), lambda i, *_: (pi(i), 0)),
                  pl.BlockSpec((tt, D), lambda i, *_: (0, 0)),
                  pl.BlockSpec((None, 1, D), lambda i, *_: ((pi(i) * tt) // rpg, 0, 0)),
                  pl.BlockSpec((None, tt, D), lambda i, *_: (0, 0, 0)),
                  pl.BlockSpec((tt, 128), lambda i, *_: (i, 0))],
        out_specs=[pl.BlockSpec((tt, D), lambda i, *_: (pi(i), 0)),
                   pl.BlockSpec((tt, D), lambda i, *_: (0, 0))],
        scratch_shapes=[pltpu.VMEM((2, tt, D), F32), pltpu.SemaphoreType.DMA(())],
    )
    return pl.pallas_call(
        functools.partial(_combine_kernel, tt=tt, np_tiles=np_tiles),
        grid_spec=grid_spec,
        out_shape=[jax.ShapeDtypeStruct((tp, D), F32), jax.ShapeDtypeStruct((ts, D), F32)],
        compiler_params=_cp(("arbitrary",), 32),
        name="moe_combine",
    )(pos1, pos2, y, xp2d, xs2d, gate_p, gate_s, wt)


def _moe(layer, xp2d, xs2d, modp, mods, g, wr, wg, wu, wd):
    tp, ts = xp2d.shape[0], xs2d.shape[0]
    wr_pad = jnp.pad(wr, ((0, 0), (0, 128 - NE))).astype(BF16)
    hp, ip, wtp = _router(xp2d, g, modp["sc2"], modp["sh2"], modp["rpg"], 1, wr_pad, 256)
    hsm, is_, wts = _router(xs2d, g, mods["sc2"], mods["sh2"], ts, ts, wr_pad, ts)
    h_all = jnp.concatenate([hp, hsm], axis=0)
    idx = jnp.concatenate([ip, is_], axis=0)
    wt = jnp.concatenate([wtp, wts], axis=0)
    nc = (2 * (tp + ts)) // MOE_CHUNK + NE
    pos, src, cid, ce, nsub = _moe_plan(idx[:, :2], nc)
    hs = _gather_rows(h_all, src, nsub, nc)
    y = _moe_ffn(layer, hs, cid, ce, nsub, wg, wu, wd, nc)
    return _combine(y, pos[0::2], pos[1::2], xp2d, xs2d, modp["g2"], mods["g2"], modp["rpg"], wt)


def kernel(x_prompt, x_sample, cache_k, cache_v, cache_logf, page_table, c_prompt, c_sample, norm1_g, norm2_g, w_ada, b_ada, w_in, b_forget, q_norm_g, k_norm_g, sgu_ln_g, sgu_ln_b, sgu_w, sgu_b, w_branch_attn, w_branch_sgu, w_out, ffd_w_gate, ffd_w_up, ffd_w_down, w_router, moe_w_gate, moe_w_up, moe_w_down):
    nb, seq, _ = x_prompt.shape
    db, nd, _ = x_sample.shape
    depth, npool = cache_k.shape[:2]
    tp, ts = nb * seq, db * nd

    c_all = jnp.concatenate([c_prompt, c_sample], axis=0)
    mc = -(-c_all.shape[0] // 8) * 8
    mod_all = _ada(jnp.pad(c_all, ((0, mc - c_all.shape[0]), (0, 0))), w_ada, b_ada)

    cache_k2 = cache_k.reshape(depth, npool, CH * NH, DH)
    cache_v2 = cache_v.reshape(depth, npool, CH * NH, DH)
    rloc, tot = _logf_pages(cache_logf.reshape(depth, npool, CH * NH))

    cut = 3 * AW
    wp_all = jnp.concatenate([w_in[:, :, :cut], w_in[:, :, cut + NH:]], axis=2).astype(BF16)
    wf_all = jnp.pad(w_in[:, :, cut:cut + NH], ((0, 0), (0, 0), (0, 128 - NH))).astype(BF16)
    wa_all, ws_all, wo_all = w_branch_attn.astype(BF16), w_branch_sgu.astype(BF16), w_out.astype(BF16)
    fpad = -ffd_w_gate.shape[2] % FFN_TF
    fwg = jnp.pad(ffd_w_gate, ((0, 0), (0, 0), (0, fpad))).astype(BF16)
    fwu = jnp.pad(ffd_w_up, ((0, 0), (0, 0), (0, fpad))).astype(BF16)
    fwd = jnp.pad(ffd_w_down, ((0, 0), (0, fpad), (0, 0))).astype(BF16)

    eye_seq = jnp.eye(db, dtype=F32)
    xp, xs = x_prompt.reshape(tp, D), x_sample.reshape(ts, D)
    outs = {k: [] for k in ("kp", "vp", "fp", "ks", "vs", "fs", "ss")}
    for l in range(depth):
        names = ("sh1", "sc1", "g1", "sh2", "sc2", "g2")
        modp = {n: mod_all[l, :nb, i * D:(i + 1) * D][:, None, :] for i, n in enumerate(names)}
        mods = {n: jnp.repeat(mod_all[l, nb:nb + db, i * D:(i + 1) * D], nd, axis=0)[None] for i, n in enumerate(names)}
        modp["rpg"] = seq

        bf = jnp.pad(b_forget[l], (0, 128 - NH))[None]
        par = (wp_all, wf_all, bf, q_norm_g[l][None], k_norm_g[l][None])
        lng, lnb = sgu_ln_g[l][None], sgu_ln_b[l][None]
        n1 = norm1_g[l][None]
        n2 = norm2_g[l][None]

        q, k, v, lf, rest = _inproj(l, xp, n1, modp["sc1"], modp["sh1"], seq, 1, *par, tm=1024,
                                    qscale=DH ** -0.5 * LOG2E)
        lf = lf[:, :NH]
        crow = _cumsum(lf.reshape(nb, seq, NH).transpose(0, 2, 1), LOG2E)
        o = _flash(q.reshape(nb, seq, AW), k.reshape(nb, seq, AW), v.reshape(nb, seq, AW),
                   crow.transpose(0, 2, 1), crow)
        bmat = jnp.broadcast_to(sgu_b[l][:, :, None], (NG, CH, CH))
        mg, _ = _mix(l, o.reshape(tp, AW), rest, sgu_w[l], bmat, lng, lnb, wa_all, ws_all, 512)
        xp = _outproj(l, mg, wo_all, xp, modp["g1"], seq, 1, 1024)
        outs["kp"].append(k.reshape(nb, seq, NH, DH))
        outs["vp"].append(v.reshape(nb, seq, NH, DH))
        outs["fp"].append(lf.reshape(nb, seq, NH))

        q, k, v, lf, rest = _inproj(l, xs, n1, mods["sc1"], mods["sh1"], ts, ts, *par, tm=ts, qscale=DH ** -0.5)
        lf = lf[:, :NH]
        q_hq = q.reshape(db, nd, NH, DH).transpose(0, 2, 1, 3).reshape(db, NH * nd, DH)
        o = _paged(l, page_table, q_hq, k.reshape(db, nd * NH, DH), v.reshape(db, nd * NH, DH),
                   lf.reshape(db, 1, nd * NH), cache_k2, cache_v2, rloc, tot)
        o = o.reshape(db, NH, nd, DH).transpose(0, 2, 1, 3).reshape(ts, AW)
        w4 = sgu_w[l][:, :nd, :nd]
        wmix = (eye_seq[None, :, None, :, None] * w4[:, None, :, None, :]).reshape(NG, ts, ts)
        bmat = jnp.broadcast_to(jnp.tile(sgu_b[l][:, :nd], (1, db))[:, :, None], (NG, CH, CH))
        mg, vsn = _mix(l, o, rest, wmix, bmat, lng, lnb, wa_all, ws_all, ts)
        xs = _outproj(l, mg, wo_all, xs, mods["g1"], ts, ts, ts)
        outs["ks"].append(k.reshape(db, nd, NH, DH))
        outs["vs"].append(v.reshape(db, nd, NH, DH))
        outs["fs"].append(lf.reshape(db, nd, NH))
        outs["ss"].append(vsn.reshape(db, nd, SW))

        j = l // 2
        if l % 2 == 0:
            xp = _ffn_dense(j, xp, n2, modp["sc2"], modp["sh2"], modp["g2"], seq, 1, fwg, fwu, fwd, tm=512,
                            vmem_mb=48)
            xs = _ffn_dense(j, xs, n2, mods["sc2"], mods["sh2"], mods["g2"], ts, ts, fwg, fwu, fwd, tm=ts,
                            vmem_mb=32)
        else:
            xp, xs = _moe(j, xp, xs, modp, mods, n2, w_router[j], moe_w_gate, moe_w_up, moe_w_down)

    st = lambda n: jnp.stack(outs[n])
    return (xp.reshape(nb, seq, D), xs.reshape(db, nd, D), st("kp"), st("vp"), st("fp"),
            st("ks"), st("vs"), st("fs"), st("ss"))
```

```python
import functools
import math

import jax
import jax.numpy as jnp
from jax import lax
from jax.experimental import pallas as pl
from jax.experimental.pallas import tpu as pltpu

F32, BF16, I32 = jnp.float32, jnp.bfloat16, jnp.int32

D = 2048
NH, DH = 8, 128
AW = NH * DH
SW = 1024
NG, GW = 8, 128
CH = 128
NE = 8
EPS = 1e-6
NEG = -1e30
LOG2E = math.log2(math.e)

MOE_CHUNK = 1280
MOE_SUB = 256
MOE_GRAN = 128
MOE_BODY = 512
PAGES_PER_STEP = 8
FFN_TF = 512


def _cp(sem, mb):
    return pltpu.CompilerParams(dimension_semantics=sem, vmem_limit_bytes=mb << 20)


def _normmod(x, g, sc, sh):
    y = x * lax.rsqrt(jnp.mean(x * x, axis=-1, keepdims=True) + EPS)
    return (y * g) * (1.0 + sc) + sh


def _split3(x):
    x1 = x.astype(BF16)
    r1 = x - x1.astype(F32)
    x2 = r1.astype(BF16)
    r2 = r1 - x2.astype(F32)
    return x1, x2, r2.astype(BF16)


def _dot3(x, m):
    return sum(jnp.dot(p, m, preferred_element_type=F32) for p in _split3(x))


def _dot_nt(a, b):
    return lax.dot_general(a, b, (((1,), (1,)), ((), ())), preferred_element_type=F32)


def _silu(x):
    return x * jax.nn.sigmoid(x)


def _mod_spec(tm, rows_per_group, r):
    return pl.BlockSpec((None, r, D), lambda i, *_: ((i * tm) // rows_per_group, 0, 0))


def _ada_kernel(c_ref, w_ref, b_ref, o_ref):
    a = _silu(c_ref[...]).astype(BF16)
    o_ref[...] = jnp.dot(a, w_ref[...].astype(BF16), preferred_element_type=F32) + b_ref[...]


def _ada(c_all, w_ada, b_ada):
    nl, _, n = w_ada.shape
    mc = c_all.shape[0]
    tn = 1024
    return pl.pallas_call(
        _ada_kernel,
        grid=(nl, n // tn),
        in_specs=[pl.BlockSpec((mc, D), lambda l, j: (0, 0)),
                  pl.BlockSpec((None, D, tn), lambda l, j: (l, 0, j)),
                  pl.BlockSpec((None, 1, tn), lambda l, j: (l, 0, j))],
        out_specs=pl.BlockSpec((None, mc, tn), lambda l, j: (l, 0, j)),
        out_shape=jax.ShapeDtypeStruct((nl, mc, n), F32),
        compiler_params=_cp(("parallel", "parallel"), 40),
        name="ada",
    )(c_all, w_ada, b_ada.reshape(nl, 1, n))


_TN_IN = 512
_J_Q, _J_K, _J_V, _J_R, _J_END = 0, 2, 4, 6, 18
_R_U, _R_VS, _R_GA, _R_GB = 0, SW, 2 * SW, 2 * SW + D


def _headnorm(z, g):
    outs = []
    for hh in range(z.shape[1] // DH):
        zz = z[:, hh * DH:(hh + 1) * DH]
        outs.append(zz * lax.rsqrt(jnp.mean(zz * zz, axis=-1, keepdims=True) + EPS) * g)
    return jnp.concatenate(outs, axis=-1)


def _inproj_kernel(x_ref, g_ref, sc_ref, sh_ref, w_ref, wf_ref, bf_ref, qg_ref, kg_ref,
                   q_ref, k_ref, v_ref, lf_ref, r_ref, h_scr, *, qscale):
    j = pl.program_id(1)

    @pl.when(j == 0)
    def _():
        h = _normmod(x_ref[...], g_ref[...], sc_ref[...], sh_ref[...]).astype(BF16)
        h_scr[...] = h
        f = jnp.dot(h, wf_ref[...], preferred_element_type=F32) + bf_ref[...]
        lf_ref[...] = jnp.minimum(f, 0.0) - jnp.log1p(jnp.exp(-jnp.abs(f)))

    z = jnp.dot(h_scr[...], w_ref[...], preferred_element_type=F32)
    r_ref[...] = z

    @pl.when(j < _J_K)
    def _():
        q_ref[...] = (_headnorm(z, qg_ref[...]) * qscale).astype(BF16)

    @pl.when((j >= _J_K) & (j < _J_V))
    def _():
        k_ref[...] = _headnorm(z, kg_ref[...])

    @pl.when((j >= _J_V) & (j < _J_R))
    def _():
        v_ref[...] = z


def _inproj(layer, x2d, g, sc, sh, rpg, r, wp, wf, bf, qg, kg, tm, qscale):
    t = x2d.shape[0]
    tn = _TN_IN

    def cspec(j0, nb):
        return pl.BlockSpec((tm, tn), lambda i, j: (i, jnp.clip(j - j0, 0, nb - 1)))

    full = lambda a: pl.BlockSpec(a.shape, lambda i, j: (0,) * a.ndim)
    return pl.pallas_call(
        functools.partial(_inproj_kernel, qscale=qscale),
        grid=(t // tm, _J_END),
        in_specs=[pl.BlockSpec((tm, D), lambda i, j: (i, 0)), full(g), _mod_spec(tm, rpg, r), _mod_spec(tm, rpg, r),
                  pl.BlockSpec((None, D, tn), lambda i, j: (layer, 0, j)),
                  pl.BlockSpec((None, D, 128), lambda i, j: (layer, 0, 0)),
                  full(bf), full(qg), full(kg)],
        out_specs=[cspec(_J_Q, 2), cspec(_J_K, 2), cspec(_J_V, 2),
                   pl.BlockSpec((tm, 128), lambda i, j: (i, 0)),
                   cspec(_J_R, _J_END - _J_R)],
        out_shape=[jax.ShapeDtypeStruct((t, AW), BF16), jax.ShapeDtypeStruct((t, AW), F32),
                   jax.ShapeDtypeStruct((t, AW), F32), jax.ShapeDtypeStruct((t, 128), F32),
                   jax.ShapeDtypeStruct((t, 2 * SW + 2 * D), F32)],
        scratch_shapes=[pltpu.VMEM((tm, D), BF16)],
        compiler_params=_cp(("parallel", "arbitrary"), 52),
        name="inproj",
    )(x2d, g, sc, sh, wp, wf, bf, qg, kg)


def _cumsum_kernel(x_ref, o_ref, *, scale):
    s = x_ref.shape[-1]
    r = lax.broadcasted_iota(I32, (128, 128), 0)
    c = lax.broadcasted_iota(I32, (128, 128), 1)
    tri = jnp.where(r <= c, 1.0, 0.0).astype(BF16)
    carry = jnp.zeros((NH, 1), F32)
    for b in range(s // 128):
        y = _dot3(x_ref[:, b * 128:(b + 1) * 128], tri) + carry
        o_ref[:, b * 128:(b + 1) * 128] = y * scale
        carry = y[:, 127:128]


def _cumsum(lft, scale):
    b, _, s = lft.shape
    return pl.pallas_call(
        functools.partial(_cumsum_kernel, scale=scale),
        grid=(b,),
        in_specs=[pl.BlockSpec((None, NH, s), lambda i: (i, 0, 0))],
        out_specs=pl.BlockSpec((None, NH, s), lambda i: (i, 0, 0)),
        out_shape=jax.ShapeDtypeStruct((b, NH, s), F32),
        compiler_params=_cp(("parallel",), 32),
        name="logf_cumsum",
    )(lft)


def _flash_kernel(q_ref, k_ref, v_ref, cc_ref, cr_ref, o_ref, *, t):
    h = pl.program_id(1)
    i = pl.program_id(2)
    q = q_ref[...]
    lane = lax.broadcasted_iota(I32, (t, NH), 1)
    cq = jnp.sum(jnp.where(lane == h, cc_ref[...], 0.0), axis=-1, keepdims=True)

    def step(kb, carry, masked):
        m, l, acc = carry
        k0 = pl.multiple_of(kb * t, t)
        k = k_ref[pl.ds(k0, t), :].astype(BF16)
        v = v_ref[pl.ds(k0, t), :].astype(BF16)
        ck = cr_ref[pl.ds(h, 1), pl.ds(k0, t)]
        s = _dot_nt(q, k) + cq - ck
        if masked:
            row = lax.broadcasted_iota(I32, (t, t), 0)
            col = lax.broadcasted_iota(I32, (t, t), 1)
            s = jnp.where(col <= row, s, NEG)
        m_new = jnp.maximum(m, jnp.max(s, axis=-1, keepdims=True))
        alpha = jnp.exp2(m - m_new)
        p = jnp.exp2(s - m_new)
        l = alpha * l + jnp.sum(p, axis=-1, keepdims=True)
        acc = alpha * acc + jnp.dot(p.astype(BF16), v, preferred_element_type=F32)
        return m_new, l, acc

    init = (jnp.full((t, 1), NEG, F32), jnp.zeros((t, 1), F32), jnp.zeros((t, DH), F32))
    carry = lax.fori_loop(0, i, lambda kb, c: step(kb, c, False), init)
    _, l, acc = step(i, carry, True)
    o_ref[...] = (acc / l).astype(BF16)


def _flash(q, k, v, ccol, crow, t=512):
    b, s, _ = q.shape
    return pl.pallas_call(
        functools.partial(_flash_kernel, t=t),
        grid=(b, NH, s // t),
        in_specs=[pl.BlockSpec((None, t, DH), lambda bb, h, i: (bb, i, h)),
                  pl.BlockSpec((None, s, DH), lambda bb, h, i: (bb, 0, h)),
                  pl.BlockSpec((None, s, DH), lambda bb, h, i: (bb, 0, h)),
                  pl.BlockSpec((None, t, NH), lambda bb, h, i: (bb, i, 0)),
                  pl.BlockSpec((None, NH, s), lambda bb, h, i: (bb, 0, 0))],
        out_specs=pl.BlockSpec((None, t, DH), lambda bb, h, i: (bb, i, h)),
        out_shape=jax.ShapeDtypeStruct((b, s, AW), BF16),
        compiler_params=_cp(("parallel", "parallel", "arbitrary"), 40),
        name="fox_prompt",
    )(q, k, v, ccol, crow)


def _logf_pages_kernel(x_ref, ms_ref, mt_ref, r_ref, t_ref):
    p1, p2, p3 = _split3(x_ref[...])
    ms, mt = ms_ref[...], mt_ref[...]
    dot = lambda a, m: jnp.dot(a, m, preferred_element_type=F32)
    r_ref[...] = dot(p1, ms) + dot(p2, ms) + dot(p3, ms)
    t_ref[...] = dot(p1, mt) + dot(p2, mt) + dot(p3, mt)


def _logf_pages(lf_flat):
    nl, npool, w = lf_flat.shape
    tp = math.gcd(npool, 256)
    a = jnp.arange(w)
    same_head = (a[:, None] % NH) == (a[None, :] % NH)
    ms = (same_head & (a[:, None] // NH > a[None, :] // NH)).astype(BF16)
    mt = same_head.astype(BF16)
    cm = pl.BlockSpec((w, w), lambda l, i: (0, 0))
    blk = pl.BlockSpec((None, tp, w), lambda l, i: (l, i, 0))
    return pl.pallas_call(
        _logf_pages_kernel,
        grid=(nl, npool // tp),
        in_specs=[blk, cm, cm],
        out_specs=[blk, blk],
        out_shape=[jax.ShapeDtypeStruct((nl, npool, w), F32)] * 2,
        compiler_params=_cp(("parallel", "parallel"), 32),
        name="logf_pages",
    )(lf_flat, ms, mt)


def _paged_kernel(pt_ref, q_ref, kn_ref, vn_ref, lfn_ref, *refs, g, npages):
    ks, vs, rl, tt = refs[:g], refs[g:2 * g], refs[2 * g:3 * g], refs[3 * g:4 * g]
    o_ref = refs[4 * g]
    m_scr, l_scr, acc_scr, carry_scr, cn_scr = refs[4 * g + 1:]
    b = pl.program_id(0)
    s = pl.program_id(1)
    ns = pl.num_programs(1)
    nq = 4 * NH
    w = CH * NH
    q = q_ref[...]

    @pl.when(s == 0)
    def _():
        r = lax.broadcasted_iota(I32, (nq, nq), 0)
        c = lax.broadcasted_iota(I32, (nq, nq), 1)
        lf = lfn_ref[...]
        m_kh = jnp.where((r % NH == c % NH) & (r // NH <= c // NH), 1.0, 0.0).astype(BF16)
        m_hq = jnp.where((r % NH == c // 4) & (r // NH <= c % 4), 1.0, 0.0).astype(BF16)
        cn_kh = _dot3(lf, m_kh)
        cn_hq = _dot3(lf, m_hq)
        cn_col = jnp.sum(jnp.where(r == c, jnp.broadcast_to(cn_hq, (nq, nq)), 0.0), axis=1, keepdims=True)
        sn = _dot_nt(q, kn_ref[...].astype(BF16)) + cn_col - cn_kh
        sn = jnp.where((r // 4 == c % NH) & (c // NH <= r % 4), sn, NEG)
        m = jnp.max(sn, axis=-1, keepdims=True)
        p = jnp.exp(sn - m)
        m_scr[...] = jnp.broadcast_to(m, m_scr.shape)
        l_scr[...] = jnp.broadcast_to(jnp.sum(p, axis=-1, keepdims=True), l_scr.shape)
        acc_scr[...] = jnp.dot(p.astype(BF16), vn_ref[...].astype(BF16), preferred_element_type=F32)
        cn_scr[...] = jnp.broadcast_to(cn_col, cn_scr.shape)
        carry_scr[...] = jnp.zeros_like(carry_scr)

    row = lax.broadcasted_iota(I32, (nq, w), 0)
    col = lax.broadcasted_iota(I32, (nq, w), 1)
    head_ok = (row // 4) == (col % NH)
    cn_col = cn_scr[:, 0:1]
    m, l, carry = m_scr[:, 0:1], l_scr[:, 0:1], carry_scr[...]
    sts = []
    for gi in reversed(range(g)):
        page = pt_ref[b * npages + (ns - 1 - s) * g + gi]
        sub = page % 8
        bias = rl[gi][pl.ds(sub, 1), :] + carry
        carry = carry + tt[gi][pl.ds(sub, 1), :]
        st = _dot_nt(q, ks[gi][...].astype(BF16)) + bias + cn_col
        sts.append((gi, jnp.where(head_ok, st, NEG)))
    m_new = jnp.maximum(m, jnp.max(functools.reduce(jnp.maximum, [st for _, st in sts]), axis=-1, keepdims=True))
    alpha = jnp.exp(m - m_new)
    ps = [(gi, jnp.exp(st - m_new)) for gi, st in sts]
    l = alpha * l + jnp.sum(functools.reduce(jnp.add, [p for _, p in ps]), axis=-1, keepdims=True)
    pv = [jnp.dot(p.astype(BF16), vs[gi][...].astype(BF16), preferred_element_type=F32) for gi, p in ps]
    acc = alpha * acc_scr[...] + functools.reduce(jnp.add, pv)
    m_scr[...] = jnp.broadcast_to(m_new, m_scr.shape)
    l_scr[...] = jnp.broadcast_to(l, l_scr.shape)
    acc_scr[...] = acc
    carry_scr[...] = carry

    @pl.when(s == ns - 1)
    def _():
        o_ref[...] = (acc / l).astype(BF16)


def _paged(layer, page_table, q_hq, kn, vn, lfn, cache_k2, cache_v2, rloc, tot):
    db, nq, _ = q_hq.shape
    npages = page_table.shape[1]
    g = PAGES_PER_STEP
    ns = npages // g
    w = CH * NH

    def page_of(b, s, pt, gi):
        return pt[b * npages + (ns - 1 - s) * g + gi]

    per_seq = lambda shp: pl.BlockSpec((None,) + shp, lambda b, s, pt: (b, 0, 0))
    kv_specs = [pl.BlockSpec((None, None, w, DH), lambda b, s, pt, gi=gi: (layer, page_of(b, s, pt, gi), 0, 0))
                for gi in range(g)]
    lf_specs = [pl.BlockSpec((None, 8, w), lambda b, s, pt, gi=gi: (layer, page_of(b, s, pt, gi) // 8, 0))
                for gi in range(g)]
    grid_spec = pltpu.PrefetchScalarGridSpec(
        num_scalar_prefetch=1,
        grid=(db, ns),
        in_specs=[per_seq((nq, DH)), per_seq((nq, DH)), per_seq((nq, DH)), per_seq((1, nq))]
        + kv_specs + kv_specs + lf_specs + lf_specs,
        out_specs=per_seq((nq, DH)),
        scratch_shapes=[pltpu.VMEM((nq, 128), F32), pltpu.VMEM((nq, 128), F32), pltpu.VMEM((nq, DH), F32),
                        pltpu.VMEM((1, w), F32), pltpu.VMEM((nq, 128), F32)],
    )
    return pl.pallas_call(
        functools.partial(_paged_kernel, g=g, npages=npages),
        grid_spec=grid_spec,
        out_shape=jax.ShapeDtypeStruct((db, nq, DH), BF16),
        compiler_params=_cp(("parallel", "arbitrary"), 48),
        name="fox_decode",
    )(page_table.reshape(-1), q_hq, kn, vn, lfn, *([cache_k2] * g), *([cache_v2] * g), *([rloc] * g), *([tot] * g))


def _mix_kernel(oa_ref, u_ref, vs_ref, ga_ref, gb_ref, w_ref, b_ref, lng_ref, lnb_ref, wa_ref, ws_ref,
                o_ref, vsn_ref, sg_scr):
    j = pl.program_id(1)
    tm = u_ref.shape[0]

    @pl.when(j == 0)
    def _():
        t = jax.nn.gelu(vs_ref[...])
        mu = jnp.mean(t, axis=-1, keepdims=True)
        tc = t - mu
        var = jnp.mean(tc * tc, axis=-1, keepdims=True)
        vsn_ref[...] = tc * lax.rsqrt(var + EPS) * lng_ref[...] + lnb_ref[...]
        r = lax.broadcasted_iota(I32, (CH, CH), 0)
        c = lax.broadcasted_iota(I32, (CH, CH), 1)
        for gi in range(NG):
            wg = jnp.where(c <= r, w_ref[gi], 0.0).astype(BF16)
            bg = b_ref[gi]
            cs = slice(gi * GW, (gi + 1) * GW)
            for ci in range(tm // CH):
                rs = slice(ci * CH, (ci + 1) * CH)
                mixed = jnp.dot(wg, vsn_ref[rs, cs].astype(BF16), preferred_element_type=F32) + bg
                sg_scr[rs, cs] = (jax.nn.gelu(u_ref[rs, cs]) * mixed).astype(BF16)

    a = jnp.dot(oa_ref[...], wa_ref[...], preferred_element_type=F32)
    s = jnp.dot(sg_scr[...], ws_ref[...], preferred_element_type=F32)
    o_ref[...] = (jax.nn.sigmoid(ga_ref[...]) * a + jax.nn.sigmoid(gb_ref[...]) * s).astype(BF16)


def _mix(layer, oa, rest, wmix, bmat, lng, lnb, wa, ws, tm, tn=512):
    t = oa.shape[0]
    row = lambda off: pl.BlockSpec((tm, SW), lambda i, j: (i, off // SW))
    gate = lambda off: pl.BlockSpec((tm, tn), lambda i, j: (i, off // tn + j))
    par = pl.BlockSpec((NG, CH, CH), lambda i, j: (0, 0, 0))
    vec = pl.BlockSpec((1, SW), lambda i, j: (0, 0))
    w_spec = pl.BlockSpec((None, AW, tn), lambda i, j: (layer, 0, j))
    return pl.pallas_call(
        _mix_kernel,
        grid=(t // tm, D // tn),
        in_specs=[pl.BlockSpec((tm, AW), lambda i, j: (i, 0)), row(_R_U), row(_R_VS), gate(_R_GA), gate(_R_GB),
                  par, par, vec, vec, w_spec, w_spec],
        out_specs=[pl.BlockSpec((tm, tn), lambda i, j: (i, j)), pl.BlockSpec((tm, SW), lambda i, j: (i, 0))],
        out_shape=[jax.ShapeDtypeStruct((t, D), BF16), jax.ShapeDtypeStruct((t, SW), F32)],
        scratch_shapes=[pltpu.VMEM((tm, SW), BF16)],
        compiler_params=_cp(("parallel", "arbitrary"), 48),
        name="mix",
    )(oa, rest, rest, rest, rest, wmix, bmat, lng, lnb, wa, ws)


def _outproj_kernel(a_ref, w_ref, x_ref, g_ref, o_ref):
    y = jnp.dot(a_ref[...], w_ref[...], preferred_element_type=F32)
    o_ref[...] = x_ref[...] + g_ref[...] * y


def _outproj(layer, a, w, x2d, gate, rpg, r, tm, tn=512):
    t = a.shape[0]
    return pl.pallas_call(
        _outproj_kernel,
        grid=(t // tm, D // tn),
        in_specs=[pl.BlockSpec((tm, D), lambda i, j: (i, 0)),
                  pl.BlockSpec((None, D, tn), lambda i, j: (layer, 0, j)),
                  pl.BlockSpec((tm, tn), lambda i, j: (i, j)),
                  pl.BlockSpec((None, r, tn), lambda i, j: ((i * tm) // rpg, 0, j))],
        out_specs=pl.BlockSpec((tm, tn), lambda i, j: (i, j)),
        out_shape=jax.ShapeDtypeStruct((t, D), F32),
        compiler_params=_cp(("parallel", "parallel"), 48),
        name="outproj",
    )(a, w, x2d, gate)


def _ffn_kernel(x_ref, g_ref, sc_ref, sh_ref, gate_ref, wg_ref, wu_ref, wd_ref, o_ref, h_scr, *, tail):
    f = pl.program_id(1)
    last = pl.num_programs(1) - 1
    tf = wg_ref.shape[1]

    @pl.when(f == 0)
    def _():
        h_scr[...] = _normmod(x_ref[...], g_ref[...], sc_ref[...], sh_ref[...]).astype(BF16)
        o_ref[...] = jnp.zeros_like(o_ref)

    def accumulate(valid):
        h = h_scr[...]
        gg = jnp.dot(h, wg_ref[...], preferred_element_type=F32)
        uu = jnp.dot(h, wu_ref[...], preferred_element_type=F32)
        a = _silu(gg) * uu
        wd = wd_ref[...]
        if valid < tf:
            a = jnp.where(lax.broadcasted_iota(I32, a.shape, 1) < valid, a, 0.0)
            wd = jnp.where(lax.broadcasted_iota(I32, wd.shape, 0) < valid, wd, jnp.zeros_like(wd))
        o_ref[...] += jnp.dot(a.astype(BF16), wd, preferred_element_type=F32)

    if tail == tf:
        accumulate(tf)
    else:
        @pl.when(f < last)
        def _():
            accumulate(tf)

        @pl.when(f == last)
        def _():
            accumulate(tail)

    @pl.when(f == last)
    def _():
        o_ref[...] = x_ref[...] + gate_ref[...] * o_ref[...]


def _ffn_dense(layer, x2d, g, sc, sh, gate, rpg, r, wg, wu, wd, tm, vmem_mb):
    t = x2d.shape[0]
    tf = FFN_TF
    dff = wg.shape[2]
    nf = pl.cdiv(dff, tf)
    full = lambda a: pl.BlockSpec(a.shape, lambda i, j: (0,) * a.ndim)
    ms = _mod_spec(tm, rpg, r)
    return pl.pallas_call(
        functools.partial(_ffn_kernel, tail=dff - (nf - 1) * tf),
        grid=(t // tm, nf),
        in_specs=[pl.BlockSpec((tm, D), lambda i, j: (i, 0)), full(g), ms, ms, ms,
                  pl.BlockSpec((None, D, tf), lambda i, j: (layer, 0, j)),
                  pl.BlockSpec((None, D, tf), lambda i, j: (layer, 0, j)),
                  pl.BlockSpec((None, tf, D), lambda i, j: (layer, j, 0))],
        out_specs=pl.BlockSpec((tm, D), lambda i, j: (i, 0)),
        out_shape=jax.ShapeDtypeStruct((t, D), F32),
        scratch_shapes=[pltpu.VMEM((tm, D), BF16)],
        compiler_params=_cp(("parallel", "arbitrary"), vmem_mb),
        name="ffn_dense",
    )(x2d, g, sc, sh, gate, wg, wu, wd)


def _router_kernel(x_ref, g_ref, sc_ref, sh_ref, wr_ref, h_ref, idx_ref, wt_ref):
    h = _normmod(x_ref[...], g_ref[...], sc_ref[...], sh_ref[...])
    h_ref[...] = h
    logits = jnp.dot(h.astype(BF16), wr_ref[...], preferred_element_type=F32)
    ls = [logits[:, e:e + 1] for e in range(NE)]
    m1 = functools.reduce(jnp.maximum, ls)
    i1 = jnp.full(m1.shape, NE, I32)
    for e in reversed(range(NE)):
        i1 = jnp.where(ls[e] == m1, e, i1)
    ls2 = [jnp.where(i1 == e, -jnp.inf, ls[e]) for e in range(NE)]
    m2 = functools.reduce(jnp.maximum, ls2)
    i2 = jnp.full(m1.shape, NE, I32)
    for e in reversed(range(NE)):
        i2 = jnp.where(ls2[e] == m2, e, i2)
    e2 = jnp.exp(m2 - m1)
    p1 = 1.0 / (1.0 + e2)
    lane = lax.broadcasted_iota(I32, idx_ref.shape, 1)
    idx_ref[...] = jnp.where(lane == 0, i1, jnp.where(lane == 1, i2, 0))
    wt_ref[...] = jnp.where(lane == 0, p1, jnp.where(lane == 1, e2 * p1, 0.0))


def _router(x2d, g, sc, sh, rpg, r, wr_pad, tm):
    t = x2d.shape[0]
    full = lambda a: pl.BlockSpec(a.shape, lambda i: (0,) * a.ndim)
    ms = _mod_spec(tm, rpg, r)
    return pl.pallas_call(
        _router_kernel,
        grid=(t // tm,),
        in_specs=[pl.BlockSpec((tm, D), lambda i: (i, 0)), full(g), ms, ms, full(wr_pad)],
        out_specs=[pl.BlockSpec((tm, D), lambda i: (i, 0)), pl.BlockSpec((tm, 128), lambda i: (i, 0)),
                   pl.BlockSpec((tm, 128), lambda i: (i, 0))],
        out_shape=[jax.ShapeDtypeStruct((t, D), F32), jax.ShapeDtypeStruct((t, 128), I32),
                   jax.ShapeDtypeStruct((t, 128), F32)],
        compiler_params=_cp(("parallel",), 32),
        name="router",
    )(x2d, g, sc, sh, wr_pad)


def _moe_plan(idx2, nc):
    r = MOE_CHUNK
    e_flat = idx2.reshape(-1)
    npairs = e_flat.shape[0]
    onehot = (e_flat[:, None] == jnp.arange(NE, dtype=I32)[None, :]).astype(I32)
    csum = jnp.cumsum(onehot, axis=0)
    rank = jnp.sum(onehot * csum, axis=1) - 1
    counts = csum[-1]
    nch = (counts + r - 1) // r
    cstart = jnp.cumsum(nch) - nch
    n_used = jnp.sum(nch)
    chunk = jnp.sum(onehot * cstart[None, :], axis=1) + rank // r
    pos = chunk * r + rank % r
    src = jnp.zeros((nc * r,), I32).at[pos].set(jnp.arange(npairs, dtype=I32) // 2)
    cid = jnp.minimum(jnp.arange(nc, dtype=I32), n_used - 1)
    ce = jnp.sum((cid[:, None] >= cstart[None, :]).astype(I32), axis=1) - 1
    rows = jnp.clip(counts[ce] - (cid - cstart[ce]) * r, 0, r)
    rows = jnp.where(jnp.arange(nc) < n_used, rows, 0)
    ngran = (rows + MOE_GRAN - 1) // MOE_GRAN
    return pos, src, cid, ce, ngran


def _gather_kernel(src_ref, ngran_ref, h_hbm, o_ref, buf, sem):
    s = pl.program_id(0)
    ns = pl.num_programs(0)
    per_chunk = MOE_CHUNK // MOE_SUB

    def live(t):
        return (t % per_chunk) * (MOE_SUB // MOE_GRAN) < ngran_ref[t // per_chunk]

    def row_copy(slot, r, t):
        return pltpu.make_async_copy(h_hbm.at[pl.ds(t, 1)], buf.at[slot, pl.ds(r, 1)], sem.at[slot])

    def start_tile(t, slot):
        def issue(r, c):
            row_copy(slot, r, src_ref[t * MOE_SUB + r]).start()
            return c

        lax.fori_loop(0, MOE_SUB, issue, 0, unroll=8)

    @pl.when((s == 0) & live(0))
    def _():
        start_tile(0, 0)

    nxt = jnp.minimum(s + 1, ns - 1)

    @pl.when((s + 1 < ns) & live(nxt))
    def _():
        start_tile(nxt, nxt % 2)

    @pl.when(live(s))
    def _():
        slot = s % 2

        def drain(r, c):
            row_copy(slot, r, 0).wait()
            return c

        lax.fori_loop(0, MOE_SUB, drain, 0, unroll=8)
        o_ref[...] = buf[slot].astype(BF16)

    @pl.when(jnp.logical_not(live(s)))
    def _():
        o_ref[...] = jnp.zeros_like(o_ref)


def _gather_rows(h, src, ngran, nc):
    nrows = nc * MOE_CHUNK
    grid_spec = pltpu.PrefetchScalarGridSpec(
        num_scalar_prefetch=2,
        grid=(nrows // MOE_SUB,),
        in_specs=[pl.BlockSpec(memory_space=pl.ANY)],
        out_specs=pl.BlockSpec((MOE_SUB, D), lambda s, *_: (s, 0)),
        scratch_shapes=[pltpu.VMEM((2, MOE_SUB, D), F32), pltpu.SemaphoreType.DMA((2,))],
    )
    return pl.pallas_call(
        _gather_kernel,
        grid_spec=grid_spec,
        out_shape=jax.ShapeDtypeStruct((nrows, D), BF16),
        compiler_params=_cp(("arbitrary",), 32),
        name="moe_gather",
    )(src, ngran, h)


def _moe_ffn_kernel(cid_ref, ce_ref, ngran_ref, hs_ref, wg_ref, wu_ref, wd_ref, y_ref, wgb, wub, wdb):
    c = pl.program_id(0)
    f = pl.program_id(1)
    ngran = ngran_ref[c]

    @pl.when(f == 0)
    def _():
        y_ref[...] = jnp.zeros_like(y_ref)

    def block(r0, rows):
        h = hs_ref[pl.ds(r0, rows), :]
        gg = jnp.dot(h, wgb[...], preferred_element_type=F32)
        uu = jnp.dot(h, wub[...], preferred_element_type=F32)
        a = (_silu(gg) * uu).astype(BF16)
        y_ref[pl.ds(r0, rows), :] += jnp.dot(a, wdb[...], preferred_element_type=F32)

    @pl.when(ngran > 0)
    def _():
        wgb[...] = wg_ref[...].astype(BF16)
        wub[...] = wu_ref[...].astype(BF16)
        wdb[...] = wd_ref[...].astype(BF16)

        per_body = MOE_BODY // MOE_GRAN
        nbody = ngran // per_body

        def body(s, carry):
            block(pl.multiple_of(s * MOE_BODY, MOE_BODY), MOE_BODY)
            return carry

        lax.fori_loop(0, nbody, body, 0)

        rem = ngran - nbody * per_body
        base = nbody * MOE_BODY
        rows = MOE_BODY // 2
        while rows >= MOE_GRAN:
            bit = rows // MOE_GRAN
            start = base + (rem // (2 * bit)) * (2 * rows)

            @pl.when((rem // bit) % 2 == 1)
            def _(start=start, rows=rows):
                block(pl.multiple_of(start, rows), rows)

            rows //= 2


def _moe_ffn(layer, hs, cid, ce, ngran, wg, wu, wd, nc, tf=256):
    dffe = wg.shape[3]
    nf = dffe // tf
    r = MOE_CHUNK

    def fidx(c, f, ngran):
        return jnp.where(ngran[c] > 0, f, nf - 1)

    grid_spec = pltpu.PrefetchScalarGridSpec(
        num_scalar_prefetch=3,
        grid=(nc, nf),
        in_specs=[pl.BlockSpec((r, D), lambda c, f, cid, ce, ns: (cid[c], 0)),
                  pl.BlockSpec((None, None, D, tf), lambda c, f, cid, ce, ns: (layer, ce[c], 0, fidx(c, f, ns))),
                  pl.BlockSpec((None, None, D, tf), lambda c, f, cid, ce, ns: (layer, ce[c], 0, fidx(c, f, ns))),
                  pl.BlockSpec((None, None, tf, D), lambda c, f, cid, ce, ns: (layer, ce[c], fidx(c, f, ns), 0))],
        out_specs=pl.BlockSpec((r, D), lambda c, f, cid, ce, ns: (c, 0)),
        scratch_shapes=[pltpu.VMEM((D, tf), BF16), pltpu.VMEM((D, tf), BF16), pltpu.VMEM((tf, D), BF16)],
    )
    return pl.pallas_call(
        _moe_ffn_kernel,
        grid_spec=grid_spec,
        out_shape=jax.ShapeDtypeStruct((nc * r, D), F32),
        compiler_params=_cp(("arbitrary", "arbitrary"), 56),
        name="moe_ffn",
    )(cid, ce, ngran, hs, wg, wu, wd)


def _combine_kernel(p1_ref, p2_ref, y_hbm, xp_ref, xs_ref, gp_ref, gs_ref, wt_ref, op_ref, os_ref, buf, sem,
                    *, tt, np_tiles):
    i = pl.program_id(0)
    n = pl.num_programs(0)

    def row_copy(slot, k, r, t):
        return pltpu.make_async_copy(y_hbm.at[pl.ds(t, 1)], buf.at[slot, k, pl.ds(r, 1)], sem.at[slot])

    def start_tile(t, slot):
        def issue(r, c):
            row_copy(slot, 0, r, p1_ref[t * tt + r]).start()
            row_copy(slot, 1, r, p2_ref[t * tt + r]).start()
            return c

        lax.fori_loop(0, tt, issue, 0, unroll=8)

    @pl.when(i == 0)
    def _():
        start_tile(0, 0)

    @pl.when(i + 1 < n)
    def _():
        start_tile(i + 1, (i + 1) % 2)

    slot = i % 2

    def drain(r, c):
        row_copy(slot, 0, r, 0).wait()
        row_copy(slot, 1, r, 0).wait()
        return c

    lax.fori_loop(0, tt, drain, 0, unroll=8)
    wt = wt_ref[...]
    y = wt[:, 0:1] * buf[slot, 0] + wt[:, 1:2] * buf[slot, 1]

    @pl.when(i < np_tiles)
    def _():
        op_ref[...] = xp_ref[...] + gp_ref[...] * y

    @pl.when(i >= np_tiles)
    def _():
        os_ref[...] = xs_ref[...] + gs_ref[...] * y


def _combine(y, pos1, pos2, xp2d, xs2d, gate_p, gate_s, rpg, wt):
    tp, ts = xp2d.shape[0], xs2d.shape[0]
    tt = ts
    np_tiles = tp // tt
    pi = lambda i: jnp.minimum(i, np_tiles - 1)
    grid_spec = pltpu.PrefetchScalarGridSpec(
        num_scalar_prefetch=2,
        grid=(np_tiles + 1,),
        in_specs=[pl.BlockSpec(memory_space=pl.ANY),
                  pl.BlockSpec((tt, D), lambda i, *_: (pi(i), 0)),
                  pl.BlockSpec((tt, D), lambda i, *_: (0, 0)),
                  pl.BlockSpec((None, 1, D), lambda i, *_: ((pi(i) * tt) // rpg, 0, 0)),
                  pl.BlockSpec((None, tt, D), lambda i, *_: (0, 0, 0)),
                  pl.BlockSpec((tt, 128), lambda i, *_: (i, 0))],
        out_specs=[pl.BlockSpec((tt, D), lambda i, *_: (pi(i), 0)),
                   pl.BlockSpec((tt, D), lambda i, *_: (0, 0))],
        scratch_shapes=[pltpu.VMEM((2, 2, tt, D), F32), pltpu.SemaphoreType.DMA((2,))],
    )
    return pl.pallas_call(
        functools.partial(_combine_kernel, tt=tt, np_tiles=np_tiles),
        grid_spec=grid_spec,
        out_shape=[jax.ShapeDtypeStruct((tp, D), F32), jax.ShapeDtypeStruct((ts, D), F32)],
        compiler_params=_cp(("arbitrary",), 32),
        name="moe_combine",
    )(pos1, pos2, y, xp2d, xs2d, gate_p, gate_s, wt)


def _moe(layer, xp2d, xs2d, modp, mods, g, wr, wg, wu, wd):
    tp, ts = xp2d.shape[0], xs2d.shape[0]
    wr_pad = jnp.pad(wr, ((0, 0), (0, 128 - NE))).astype(BF16)
    hp, ip, wtp = _router(xp2d, g, modp["sc2"], modp["sh2"], modp["rpg"], 1, wr_pad, 256)
    hsm, is_, wts = _router(xs2d, g, mods["sc2"], mods["sh2"], ts, ts, wr_pad, ts)
    h_all = jnp.concatenate([hp, hsm], axis=0)
    idx = jnp.concatenate([ip, is_], axis=0)
    wt = jnp.concatenate([wtp, wts], axis=0)
    nc = (2 * (tp + ts)) // MOE_CHUNK + NE
    pos, src, cid, ce, ngran = _moe_plan(idx[:, :2], nc)
    hs = _gather_rows(h_all, src, ngran, nc)
    y = _moe_ffn(layer, hs, cid, ce, ngran, wg, wu, wd, nc)
    return _combine(y, pos[0::2], pos[1::2], xp2d, xs2d, modp["g2"], mods["g2"], modp["rpg"], wt)


def kernel(x_prompt, x_sample, cache_k, cache_v, cache_logf, page_table, c_prompt, c_sample, norm1_g, norm2_g, w_ada, b_ada, w_in, b_forget, q_norm_g, k_norm_g, sgu_ln_g, sgu_ln_b, sgu_w, sgu_b, w_branch_attn, w_branch_sgu, w_out, ffd_w_gate, ffd_w_up, ffd_w_down, w_router, moe_w_gate, moe_w_up, moe_w_down):
    nb, seq, _ = x_prompt.shape
    db, nd, _ = x_sample.shape
    depth, npool = cache_k.shape[:2]
    tp, ts = nb * seq, db * nd

    c_all = jnp.concatenate([c_prompt, c_sample], axis=0)
    mc = -(-c_all.shape[0] // 8) * 8
    mod_all = _ada(jnp.pad(c_all, ((0, mc - c_all.shape[0]), (0, 0))), w_ada, b_ada)

    cache_k2 = cache_k.reshape(depth, npool, CH * NH, DH)
    cache_v2 = cache_v.reshape(depth, npool, CH * NH, DH)
    rloc, tot = _logf_pages(cache_logf.reshape(depth, npool, CH * NH))

    cut = 3 * AW
    wp_all = jnp.concatenate([w_in[:, :, :cut].astype(BF16), w_in[:, :, cut + NH:].astype(BF16)], axis=2)
    wf_all = jnp.pad(w_in[:, :, cut:cut + NH], ((0, 0), (0, 0), (0, 128 - NH))).astype(BF16)
    wa_all, ws_all, wo_all = w_branch_attn.astype(BF16), w_branch_sgu.astype(BF16), w_out.astype(BF16)
    fwg, fwu, fwd = ffd_w_gate.astype(BF16), ffd_w_up.astype(BF16), ffd_w_down.astype(BF16)

    eye_seq = jnp.eye(db, dtype=F32)
    xp, xs = x_prompt.reshape(tp, D), x_sample.reshape(ts, D)
    outs = {k: [] for k in ("kp", "vp", "fp", "ks", "vs", "fs", "ss")}
    for l in range(depth):
        names = ("sh1", "sc1", "g1", "sh2", "sc2", "g2")
        modp = {n: mod_all[l, :nb, i * D:(i + 1) * D][:, None, :] for i, n in enumerate(names)}
        mods = {n: jnp.repeat(mod_all[l, nb:nb + db, i * D:(i + 1) * D], nd, axis=0)[None] for i, n in enumerate(names)}
        modp["rpg"] = seq

        bf = jnp.pad(b_forget[l], (0, 128 - NH))[None]
        par = (wp_all, wf_all, bf, q_norm_g[l][None], k_norm_g[l][None])
        lng, lnb = sgu_ln_g[l][None], sgu_ln_b[l][None]
        n1 = norm1_g[l][None]
        n2 = norm2_g[l][None]

        q, k, v, lf, rest = _inproj(l, xp, n1, modp["sc1"], modp["sh1"], seq, 1, *par, tm=1024,
                                    qscale=DH ** -0.5 * LOG2E)
        lf = lf[:, :NH]
        crow = _cumsum(lf.reshape(nb, seq, NH).transpose(0, 2, 1), LOG2E)
        o = _flash(q.reshape(nb, seq, AW), k.reshape(nb, seq, AW), v.reshape(nb, seq, AW),
                   crow.transpose(0, 2, 1), crow)
        bmat = jnp.broadcast_to(sgu_b[l][:, :, None], (NG, CH, CH))
        mg, _ = _mix(l, o.reshape(tp, AW), rest, sgu_w[l], bmat, lng, lnb, wa_all, ws_all, 512)
        xp = _outproj(l, mg, wo_all, xp, modp["g1"], seq, 1, 1024)
        outs["kp"].append(k.reshape(nb, seq, NH, DH))
        outs["vp"].append(v.reshape(nb, seq, NH, DH))
        outs["fp"].append(lf.reshape(nb, seq, NH))

        q, k, v, lf, rest = _inproj(l, xs, n1, mods["sc1"], mods["sh1"], ts, ts, *par, tm=ts, qscale=DH ** -0.5)
        lf = lf[:, :NH]
        q_hq = q.reshape(db, nd, NH, DH).transpose(0, 2, 1, 3).reshape(db, NH * nd, DH)
        o = _paged(l, page_table, q_hq, k.reshape(db, nd * NH, DH), v.reshape(db, nd * NH, DH),
                   lf.reshape(db, 1, nd * NH), cache_k2, cache_v2, rloc, tot)
        o = o.reshape(db, NH, nd, DH).transpose(0, 2, 1, 3).reshape(ts, AW)
        w4 = sgu_w[l][:, :nd, :nd]
        wmix = (eye_seq[None, :, None, :, None] * w4[:, None, :, None, :]).reshape(NG, ts, ts)
        bmat = jnp.broadcast_to(jnp.tile(sgu_b[l][:, :nd], (1, db))[:, :, None], (NG, CH, CH))
        mg, vsn = _mix(l, o, rest, wmix, bmat, lng, lnb, wa_all, ws_all, ts)
        xs = _outproj(l, mg, wo_all, xs, mods["g1"], ts, ts, ts)
        outs["ks"].append(k.reshape(db, nd, NH, DH))
        outs["vs"].append(v.reshape(db, nd, NH, DH))
        outs["fs"].append(lf.reshape(db, nd, NH))
        outs["ss"].append(vsn.reshape(db, nd, SW))

        j = l // 2
        if l % 2 == 0:
            xp = _ffn_dense(j, xp, n2, modp["sc2"], modp["sh2"], modp["g2"], seq, 1, fwg, fwu, fwd, tm=512,
                            vmem_mb=48)
            xs = _ffn_dense(j, xs, n2, mods["sc2"], mods["sh2"], mods["g2"], ts, ts, fwg, fwu, fwd, tm=ts,
                            vmem_mb=32)
        else:
            xp, xs = _moe(j, xp, xs, modp, mods, n2, w_router[j], moe_w_gate, moe_w_up, moe_w_down)

    st = lambda n: jnp.stack(outs[n])
    return (xp.reshape(nb, seq, D), xs.reshape(db, nd, D), st("kp"), st("vp"), st("fp"),
            st("ks"), st("vs"), st("fs"), st("ss"))
```

```python
import functools
import math

import jax
import jax.numpy as jnp
from jax import lax
from jax.experimental import pallas as pl
from jax.experimental.pallas import tpu as pltpu

F32, BF16, I32 = jnp.float32, jnp.bfloat16, jnp.int32

D = 2048
NH, DH = 8, 128
AW = NH * DH
SW = 1024
NG, GW = 8, 128
CH = 128
NE = 8
EPS = 1e-6
NEG = -1e30
LOG2E = math.log2(math.e)

MOE_CHUNK = 1280
MOE_SUB = 256
MOE_GRAN = 128
MOE_BODY = 512
PAGES_PER_STEP = 16
FFN_TF = 512


def _cp(sem, mb):
    return pltpu.CompilerParams(dimension_semantics=sem, vmem_limit_bytes=mb << 20)


def _normmod(x, g, sc, sh):
    y = x * lax.rsqrt(jnp.mean(x * x, axis=-1, keepdims=True) + EPS)
    return (y * g) * (1.0 + sc) + sh


def _split3(x):
    x1 = x.astype(BF16)
    r1 = x - x1.astype(F32)
    x2 = r1.astype(BF16)
    r2 = r1 - x2.astype(F32)
    return x1, x2, r2.astype(BF16)


def _dot3(x, m):
    return sum(jnp.dot(p, m, preferred_element_type=F32) for p in _split3(x))


def _dot_nt(a, b):
    return lax.dot_general(a, b, (((1,), (1,)), ((), ())), preferred_element_type=F32)


def _silu(x):
    return x * jax.nn.sigmoid(x)


def _mod_spec(tm, rows_per_group, r):
    return pl.BlockSpec((None, r, D), lambda i, *_: ((i * tm) // rows_per_group, 0, 0))


def _ada_kernel(c_ref, w_ref, b_ref, o_ref):
    a = _silu(c_ref[...]).astype(BF16)
    o_ref[...] = jnp.dot(a, w_ref[...].astype(BF16), preferred_element_type=F32) + b_ref[...]


def _ada(c_all, w_ada, b_ada):
    nl, _, n = w_ada.shape
    mc = c_all.shape[0]
    tn = 1024
    return pl.pallas_call(
        _ada_kernel,
        grid=(nl, n // tn),
        in_specs=[pl.BlockSpec((mc, D), lambda l, j: (0, 0)),
                  pl.BlockSpec((None, D, tn), lambda l, j: (l, 0, j)),
                  pl.BlockSpec((None, 1, tn), lambda l, j: (l, 0, j))],
        out_specs=pl.BlockSpec((None, mc, tn), lambda l, j: (l, 0, j)),
        out_shape=jax.ShapeDtypeStruct((nl, mc, n), F32),
        compiler_params=_cp(("parallel", "parallel"), 40),
        name="ada",
    )(c_all, w_ada, b_ada.reshape(nl, 1, n))


_TN_IN = 512
_J_Q, _J_K, _J_V, _J_R, _J_END = 0, 2, 4, 6, 18
_R_U, _R_VS, _R_GA, _R_GB = 0, SW, 2 * SW, 2 * SW + D


def _headnorm(z, g):
    outs = []
    for hh in range(z.shape[1] // DH):
        zz = z[:, hh * DH:(hh + 1) * DH]
        outs.append(zz * lax.rsqrt(jnp.mean(zz * zz, axis=-1, keepdims=True) + EPS) * g)
    return jnp.concatenate(outs, axis=-1)


def _inproj_kernel(x_ref, g_ref, sc_ref, sh_ref, w_ref, wf_ref, bf_ref, qg_ref, kg_ref,
                   q_ref, k_ref, v_ref, lf_ref, r_ref, h_scr, *, qscale):
    j = pl.program_id(1)

    @pl.when(j == 0)
    def _():
        h = _normmod(x_ref[...], g_ref[...], sc_ref[...], sh_ref[...]).astype(BF16)
        h_scr[...] = h
        f = jnp.dot(h, wf_ref[...], preferred_element_type=F32) + bf_ref[...]
        lf_ref[...] = jnp.minimum(f, 0.0) - jnp.log1p(jnp.exp(-jnp.abs(f)))

    z = jnp.dot(h_scr[...], w_ref[...], preferred_element_type=F32)
    r_ref[...] = z

    @pl.when(j < _J_K)
    def _():
        q_ref[...] = (_headnorm(z, qg_ref[...]) * qscale).astype(BF16)

    @pl.when((j >= _J_K) & (j < _J_V))
    def _():
        k_ref[...] = _headnorm(z, kg_ref[...])

    @pl.when((j >= _J_V) & (j < _J_R))
    def _():
        v_ref[...] = z


def _inproj(layer, x2d, g, sc, sh, rpg, r, wp, wf, bf, qg, kg, tm, qscale):
    t = x2d.shape[0]
    tn = _TN_IN

    def cspec(j0, nb):
        return pl.BlockSpec((tm, tn), lambda i, j: (i, jnp.clip(j - j0, 0, nb - 1)))

    full = lambda a: pl.BlockSpec(a.shape, lambda i, j: (0,) * a.ndim)
    return pl.pallas_call(
        functools.partial(_inproj_kernel, qscale=qscale),
        grid=(t // tm, _J_END),
        in_specs=[pl.BlockSpec((tm, D), lambda i, j: (i, 0)), full(g), _mod_spec(tm, rpg, r), _mod_spec(tm, rpg, r),
                  pl.BlockSpec((None, D, tn), lambda i, j: (layer, 0, j)),
                  pl.BlockSpec((None, D, 128), lambda i, j: (layer, 0, 0)),
                  full(bf), full(qg), full(kg)],
        out_specs=[cspec(_J_Q, 2), cspec(_J_K, 2), cspec(_J_V, 2),
                   pl.BlockSpec((tm, 128), lambda i, j: (i, 0)),
                   cspec(_J_R, _J_END - _J_R)],
        out_shape=[jax.ShapeDtypeStruct((t, AW), BF16), jax.ShapeDtypeStruct((t, AW), F32),
                   jax.ShapeDtypeStruct((t, AW), F32), jax.ShapeDtypeStruct((t, 128), F32),
                   jax.ShapeDtypeStruct((t, 2 * SW + 2 * D), F32)],
        scratch_shapes=[pltpu.VMEM((tm, D), BF16)],
        compiler_params=_cp(("parallel", "arbitrary"), 52),
        name="inproj",
    )(x2d, g, sc, sh, wp, wf, bf, qg, kg)


def _cumsum_kernel(x_ref, o_ref, *, scale):
    s = x_ref.shape[-1]
    r = lax.broadcasted_iota(I32, (128, 128), 0)
    c = lax.broadcasted_iota(I32, (128, 128), 1)
    tri = jnp.where(r <= c, 1.0, 0.0).astype(BF16)
    carry = jnp.zeros((NH, 1), F32)
    for b in range(s // 128):
        y = _dot3(x_ref[:, b * 128:(b + 1) * 128], tri) + carry
        o_ref[:, b * 128:(b + 1) * 128] = y * scale
        carry = y[:, 127:128]


def _cumsum(lft, scale):
    b, _, s = lft.shape
    return pl.pallas_call(
        functools.partial(_cumsum_kernel, scale=scale),
        grid=(b,),
        in_specs=[pl.BlockSpec((None, NH, s), lambda i: (i, 0, 0))],
        out_specs=pl.BlockSpec((None, NH, s), lambda i: (i, 0, 0)),
        out_shape=jax.ShapeDtypeStruct((b, NH, s), F32),
        compiler_params=_cp(("parallel",), 32),
        name="logf_cumsum",
    )(lft)


def _flash_kernel(q_ref, k_ref, v_ref, cc_ref, cr_ref, o_ref, *, t):
    h = pl.program_id(1)
    i = pl.program_id(2)
    q = q_ref[...]
    lane = lax.broadcasted_iota(I32, (t, NH), 1)
    cq = jnp.sum(jnp.where(lane == h, cc_ref[...], 0.0), axis=-1, keepdims=True)

    def step(kb, carry, masked):
        m, l, acc = carry
        k0 = pl.multiple_of(kb * t, t)
        k = k_ref[pl.ds(k0, t), :].astype(BF16)
        v = v_ref[pl.ds(k0, t), :].astype(BF16)
        ck = cr_ref[pl.ds(h, 1), pl.ds(k0, t)]
        s = _dot_nt(q, k) + cq - ck
        if masked:
            row = lax.broadcasted_iota(I32, (t, t), 0)
            col = lax.broadcasted_iota(I32, (t, t), 1)
            s = jnp.where(col <= row, s, NEG)
        m_new = jnp.maximum(m, jnp.max(s, axis=-1, keepdims=True))
        alpha = jnp.exp2(m - m_new)
        p = jnp.exp2(s - m_new)
        l = alpha * l + jnp.sum(p, axis=-1, keepdims=True)
        acc = alpha * acc + jnp.dot(p.astype(BF16), v, preferred_element_type=F32)
        return m_new, l, acc

    init = (jnp.full((t, 1), NEG, F32), jnp.zeros((t, 1), F32), jnp.zeros((t, DH), F32))
    carry = lax.fori_loop(0, i, lambda kb, c: step(kb, c, False), init)
    _, l, acc = step(i, carry, True)
    o_ref[...] = (acc / l).astype(BF16)


def _flash(q, k, v, ccol, crow, t=512):
    b, s, _ = q.shape
    return pl.pallas_call(
        functools.partial(_flash_kernel, t=t),
        grid=(b, NH, s // t),
        in_specs=[pl.BlockSpec((None, t, DH), lambda bb, h, i: (bb, i, h)),
                  pl.BlockSpec((None, s, DH), lambda bb, h, i: (bb, 0, h)),
                  pl.BlockSpec((None, s, DH), lambda bb, h, i: (bb, 0, h)),
                  pl.BlockSpec((None, t, NH), lambda bb, h, i: (bb, i, 0)),
                  pl.BlockSpec((None, NH, s), lambda bb, h, i: (bb, 0, 0))],
        out_specs=pl.BlockSpec((None, t, DH), lambda bb, h, i: (bb, i, h)),
        out_shape=jax.ShapeDtypeStruct((b, s, AW), BF16),
        compiler_params=_cp(("parallel", "parallel", "arbitrary"), 40),
        name="fox_prompt",
    )(q, k, v, ccol, crow)


def _logf_pages_kernel(x_ref, ms_ref, mt_ref, r_ref, t_ref):
    p1, p2, p3 = _split3(x_ref[...])
    ms, mt = ms_ref[...], mt_ref[...]
    dot = lambda a, m: jnp.dot(a, m, preferred_element_type=F32)
    r_ref[...] = dot(p1, ms) + dot(p2, ms) + dot(p3, ms)
    t_ref[...] = dot(p1, mt) + dot(p2, mt) + dot(p3, mt)


def _logf_pages(lf_flat):
    nl, npool, w = lf_flat.shape
    tp = math.gcd(npool, 256)
    a = jnp.arange(w)
    same_head = (a[:, None] % NH) == (a[None, :] % NH)
    ms = (same_head & (a[:, None] // NH > a[None, :] // NH)).astype(BF16)
    mt = same_head.astype(BF16)
    cm = pl.BlockSpec((w, w), lambda l, i: (0, 0))
    blk = pl.BlockSpec((None, tp, w), lambda l, i: (l, i, 0))
    return pl.pallas_call(
        _logf_pages_kernel,
        grid=(nl, npool // tp),
        in_specs=[blk, cm, cm],
        out_specs=[blk, blk],
        out_shape=[jax.ShapeDtypeStruct((nl, npool, w), F32)] * 2,
        compiler_params=_cp(("parallel", "parallel"), 32),
        name="logf_pages",
    )(lf_flat, ms, mt)


def _paged_kernel(pt_ref, q_ref, kn_ref, vn_ref, lfn_ref, *refs, g, npages):
    ks, vs, rl, tt = refs[:g], refs[g:2 * g], refs[2 * g:3 * g], refs[3 * g:4 * g]
    o_ref = refs[4 * g]
    m_scr, l_scr, acc_scr, carry_scr, cn_scr = refs[4 * g + 1:]
    b = pl.program_id(0)
    s = pl.program_id(1)
    ns = pl.num_programs(1)
    nq = 4 * NH
    w = CH * NH
    q = q_ref[...]

    @pl.when(s == 0)
    def _():
        r = lax.broadcasted_iota(I32, (nq, nq), 0)
        c = lax.broadcasted_iota(I32, (nq, nq), 1)
        lf = lfn_ref[...]
        m_kh = jnp.where((r % NH == c % NH) & (r // NH <= c // NH), 1.0, 0.0).astype(BF16)
        m_hq = jnp.where((r % NH == c // 4) & (r // NH <= c % 4), 1.0, 0.0).astype(BF16)
        cn_kh = _dot3(lf, m_kh)
        cn_hq = _dot3(lf, m_hq)
        cn_col = jnp.sum(jnp.where(r == c, jnp.broadcast_to(cn_hq, (nq, nq)), 0.0), axis=1, keepdims=True)
        sn = _dot_nt(q, kn_ref[...].astype(BF16)) + cn_col - cn_kh
        sn = jnp.where((r // 4 == c % NH) & (c // NH <= r % 4), sn, NEG)
        m = jnp.max(sn, axis=-1, keepdims=True)
        p = jnp.exp(sn - m)
        m_scr[...] = jnp.broadcast_to(m, m_scr.shape)
        l_scr[...] = jnp.broadcast_to(jnp.sum(p, axis=-1, keepdims=True), l_scr.shape)
        acc_scr[...] = jnp.dot(p.astype(BF16), vn_ref[...].astype(BF16), preferred_element_type=F32)
        cn_scr[...] = jnp.broadcast_to(cn_col, cn_scr.shape)
        carry_scr[...] = jnp.zeros_like(carry_scr)

    row = lax.broadcasted_iota(I32, (nq, w), 0)
    col = lax.broadcasted_iota(I32, (nq, w), 1)
    head_ok = (row // 4) == (col % NH)
    cn_col = cn_scr[:, 0:1]
    m, l, carry = m_scr[:, 0:1], l_scr[:, 0:1], carry_scr[...]
    sts = []
    for gi in reversed(range(g)):
        page = pt_ref[b * npages + (ns - 1 - s) * g + gi]
        sub = page % 8
        bias = rl[gi][pl.ds(sub, 1), :] + carry
        carry = carry + tt[gi][pl.ds(sub, 1), :]
        st = _dot_nt(q, ks[gi][...].astype(BF16)) + bias + cn_col
        sts.append((gi, jnp.where(head_ok, st, NEG)))
    m_new = jnp.maximum(m, jnp.max(functools.reduce(jnp.maximum, [st for _, st in sts]), axis=-1, keepdims=True))
    alpha = jnp.exp(m - m_new)
    ps = [(gi, jnp.exp(st - m_new)) for gi, st in sts]
    l = alpha * l + jnp.sum(functools.reduce(jnp.add, [p for _, p in ps]), axis=-1, keepdims=True)
    pv = [jnp.dot(p.astype(BF16), vs[gi][...].astype(BF16), preferred_element_type=F32) for gi, p in ps]
    acc = alpha * acc_scr[...] + functools.reduce(jnp.add, pv)
    m_scr[...] = jnp.broadcast_to(m_new, m_scr.shape)
    l_scr[...] = jnp.broadcast_to(l, l_scr.shape)
    acc_scr[...] = acc
    carry_scr[...] = carry

    @pl.when(s == ns - 1)
    def _():
        o_ref[...] = (acc / l).astype(BF16)


def _paged(layer, page_table, q_hq, kn, vn, lfn, cache_k2, cache_v2, rloc, tot):
    db, nq, _ = q_hq.shape
    npages = page_table.shape[1]
    g = PAGES_PER_STEP
    ns = npages // g
    w = CH * NH

    def page_of(b, s, pt, gi):
        return pt[b * npages + (ns - 1 - s) * g + gi]

    per_seq = lambda shp: pl.BlockSpec((None,) + shp, lambda b, s, pt: (b, 0, 0))
    kv_specs = [pl.BlockSpec((None, None, w, DH), lambda b, s, pt, gi=gi: (layer, page_of(b, s, pt, gi), 0, 0))
                for gi in range(g)]
    lf_specs = [pl.BlockSpec((None, 8, w), lambda b, s, pt, gi=gi: (layer, page_of(b, s, pt, gi) // 8, 0))
                for gi in range(g)]
    grid_spec = pltpu.PrefetchScalarGridSpec(
        num_scalar_prefetch=1,
        grid=(db, ns),
        in_specs=[per_seq((nq, DH)), per_seq((nq, DH)), per_seq((nq, DH)), per_seq((1, nq))]
        + kv_specs + kv_specs + lf_specs + lf_specs,
        out_specs=per_seq((nq, DH)),
        scratch_shapes=[pltpu.VMEM((nq, 128), F32), pltpu.VMEM((nq, 128), F32), pltpu.VMEM((nq, DH), F32),
                        pltpu.VMEM((1, w), F32), pltpu.VMEM((nq, 128), F32)],
    )
    return pl.pallas_call(
        functools.partial(_paged_kernel, g=g, npages=npages),
        grid_spec=grid_spec,
        out_shape=jax.ShapeDtypeStruct((db, nq, DH), BF16),
        compiler_params=_cp(("parallel", "arbitrary"), 48),
        name="fox_decode",
    )(page_table.reshape(-1), q_hq, kn, vn, lfn, *([cache_k2] * g), *([cache_v2] * g), *([rloc] * g), *([tot] * g))


def _mix_kernel(oa_ref, u_ref, vs_ref, ga_ref, gb_ref, w_ref, b_ref, lng_ref, lnb_ref, wa_ref, ws_ref,
                o_ref, vsn_ref, sg_scr):
    j = pl.program_id(1)
    tm = u_ref.shape[0]

    @pl.when(j == 0)
    def _():
        t = jax.nn.gelu(vs_ref[...])
        mu = jnp.mean(t, axis=-1, keepdims=True)
        tc = t - mu
        var = jnp.mean(tc * tc, axis=-1, keepdims=True)
        vsn_ref[...] = tc * lax.rsqrt(var + EPS) * lng_ref[...] + lnb_ref[...]
        r = lax.broadcasted_iota(I32, (CH, CH), 0)
        c = lax.broadcasted_iota(I32, (CH, CH), 1)
        for gi in range(NG):
            wg = jnp.where(c <= r, w_ref[gi], 0.0).astype(BF16)
            bg = b_ref[gi]
            cs = slice(gi * GW, (gi + 1) * GW)
            for ci in range(tm // CH):
                rs = slice(ci * CH, (ci + 1) * CH)
                mixed = jnp.dot(wg, vsn_ref[rs, cs].astype(BF16), preferred_element_type=F32) + bg
                sg_scr[rs, cs] = (jax.nn.gelu(u_ref[rs, cs]) * mixed).astype(BF16)

    a = jnp.dot(oa_ref[...], wa_ref[...], preferred_element_type=F32)
    s = jnp.dot(sg_scr[...], ws_ref[...], preferred_element_type=F32)
    o_ref[...] = (jax.nn.sigmoid(ga_ref[...]) * a + jax.nn.sigmoid(gb_ref[...]) * s).astype(BF16)


def _mix(layer, oa, rest, wmix, bmat, lng, lnb, wa, ws, tm, tn=512):
    t = oa.shape[0]
    row = lambda off: pl.BlockSpec((tm, SW), lambda i, j: (i, off // SW))
    gate = lambda off: pl.BlockSpec((tm, tn), lambda i, j: (i, off // tn + j))
    par = pl.BlockSpec((NG, CH, CH), lambda i, j: (0, 0, 0))
    vec = pl.BlockSpec((1, SW), lambda i, j: (0, 0))
    w_spec = pl.BlockSpec((None, AW, tn), lambda i, j: (layer, 0, j))
    return pl.pallas_call(
        _mix_kernel,
        grid=(t // tm, D // tn),
        in_specs=[pl.BlockSpec((tm, AW), lambda i, j: (i, 0)), row(_R_U), row(_R_VS), gate(_R_GA), gate(_R_GB),
                  par, par, vec, vec, w_spec, w_spec],
        out_specs=[pl.BlockSpec((tm, tn), lambda i, j: (i, j)), pl.BlockSpec((tm, SW), lambda i, j: (i, 0))],
        out_shape=[jax.ShapeDtypeStruct((t, D), BF16), jax.ShapeDtypeStruct((t, SW), F32)],
        scratch_shapes=[pltpu.VMEM((tm, SW), BF16)],
        compiler_params=_cp(("parallel", "arbitrary"), 48),
        name="mix",
    )(oa, rest, rest, rest, rest, wmix, bmat, lng, lnb, wa, ws)


def _outproj_kernel(a_ref, w_ref, x_ref, g_ref, o_ref):
    y = jnp.dot(a_ref[...], w_ref[...], preferred_element_type=F32)
    o_ref[...] = x_ref[...] + g_ref[...] * y


def _outproj(layer, a, w, x2d, gate, rpg, r, tm, tn=512):
    t = a.shape[0]
    return pl.pallas_call(
        _outproj_kernel,
        grid=(t // tm, D // tn),
        in_specs=[pl.BlockSpec((tm, D), lambda i, j: (i, 0)),
                  pl.BlockSpec((None, D, tn), lambda i, j: (layer, 0, j)),
                  pl.BlockSpec((tm, tn), lambda i, j: (i, j)),
                  pl.BlockSpec((None, r, tn), lambda i, j: ((i * tm) // rpg, 0, j))],
        out_specs=pl.BlockSpec((tm, tn), lambda i, j: (i, j)),
        out_shape=jax.ShapeDtypeStruct((t, D), F32),
        compiler_params=_cp(("parallel", "parallel"), 48),
        name="outproj",
    )(a, w, x2d, gate)


def _ffn_kernel(x_ref, g_ref, sc_ref, sh_ref, gate_ref, wg_ref, wu_ref, wd_ref, o_ref, h_scr, *, tail):
    f = pl.program_id(1)
    last = pl.num_programs(1) - 1
    tf = wg_ref.shape[1]

    @pl.when(f == 0)
    def _():
        h_scr[...] = _normmod(x_ref[...], g_ref[...], sc_ref[...], sh_ref[...]).astype(BF16)
        o_ref[...] = jnp.zeros_like(o_ref)

    def accumulate(valid):
        h = h_scr[...]
        gg = jnp.dot(h, wg_ref[...], preferred_element_type=F32)
        uu = jnp.dot(h, wu_ref[...], preferred_element_type=F32)
        a = _silu(gg) * uu
        wd = wd_ref[...]
        if valid < tf:
            a = jnp.where(lax.broadcasted_iota(I32, a.shape, 1) < valid, a, 0.0)
            wd = jnp.where(lax.broadcasted_iota(I32, wd.shape, 0) < valid, wd, jnp.zeros_like(wd))
        o_ref[...] += jnp.dot(a.astype(BF16), wd, preferred_element_type=F32)

    if tail == tf:
        accumulate(tf)
    else:
        @pl.when(f < last)
        def _():
            accumulate(tf)

        @pl.when(f == last)
        def _():
            accumulate(tail)

    @pl.when(f == last)
    def _():
        o_ref[...] = x_ref[...] + gate_ref[...] * o_ref[...]


def _ffn_dense(layer, x2d, g, sc, sh, gate, rpg, r, wg, wu, wd, tm, vmem_mb):
    t = x2d.shape[0]
    tf = FFN_TF
    dff = wg.shape[2]
    nf = pl.cdiv(dff, tf)
    full = lambda a: pl.BlockSpec(a.shape, lambda i, j: (0,) * a.ndim)
    ms = _mod_spec(tm, rpg, r)
    return pl.pallas_call(
        functools.partial(_ffn_kernel, tail=dff - (nf - 1) * tf),
        grid=(t // tm, nf),
        in_specs=[pl.BlockSpec((tm, D), lambda i, j: (i, 0)), full(g), ms, ms, ms,
                  pl.BlockSpec((None, D, tf), lambda i, j: (layer, 0, j)),
                  pl.BlockSpec((None, D, tf), lambda i, j: (layer, 0, j)),
                  pl.BlockSpec((None, tf, D), lambda i, j: (layer, j, 0))],
        out_specs=pl.BlockSpec((tm, D), lambda i, j: (i, 0)),
        out_shape=jax.ShapeDtypeStruct((t, D), F32),
        scratch_shapes=[pltpu.VMEM((tm, D), BF16)],
        compiler_params=_cp(("parallel", "arbitrary"), vmem_mb),
        name="ffn_dense",
    )(x2d, g, sc, sh, gate, wg, wu, wd)


def _router_kernel(x_ref, g_ref, sc_ref, sh_ref, wr_ref, h_ref, idx_ref, wt_ref):
    h = _normmod(x_ref[...], g_ref[...], sc_ref[...], sh_ref[...])
    h_ref[...] = h
    logits = jnp.dot(h.astype(BF16), wr_ref[...], preferred_element_type=F32)
    ls = [logits[:, e:e + 1] for e in range(NE)]
    m1 = functools.reduce(jnp.maximum, ls)
    i1 = jnp.full(m1.shape, NE, I32)
    for e in reversed(range(NE)):
        i1 = jnp.where(ls[e] == m1, e, i1)
    ls2 = [jnp.where(i1 == e, -jnp.inf, ls[e]) for e in range(NE)]
    m2 = functools.reduce(jnp.maximum, ls2)
    i2 = jnp.full(m1.shape, NE, I32)
    for e in reversed(range(NE)):
        i2 = jnp.where(ls2[e] == m2, e, i2)
    e2 = jnp.exp(m2 - m1)
    p1 = 1.0 / (1.0 + e2)
    lane = lax.broadcasted_iota(I32, idx_ref.shape, 1)
    idx_ref[...] = jnp.where(lane == 0, i1, jnp.where(lane == 1, i2, 0))
    wt_ref[...] = jnp.where(lane == 0, p1, jnp.where(lane == 1, e2 * p1, 0.0))


def _router(x2d, g, sc, sh, rpg, r, wr_pad, tm):
    t = x2d.shape[0]
    full = lambda a: pl.BlockSpec(a.shape, lambda i: (0,) * a.ndim)
    ms = _mod_spec(tm, rpg, r)
    return pl.pallas_call(
        _router_kernel,
        grid=(t // tm,),
        in_specs=[pl.BlockSpec((tm, D), lambda i: (i, 0)), full(g), ms, ms, full(wr_pad)],
        out_specs=[pl.BlockSpec((tm, D), lambda i: (i, 0)), pl.BlockSpec((tm, 128), lambda i: (i, 0)),
                   pl.BlockSpec((tm, 128), lambda i: (i, 0))],
        out_shape=[jax.ShapeDtypeStruct((t, D), F32), jax.ShapeDtypeStruct((t, 128), I32),
                   jax.ShapeDtypeStruct((t, 128), F32)],
        compiler_params=_cp(("parallel",), 32),
        name="router",
    )(x2d, g, sc, sh, wr_pad)


def _moe_plan(idx2, nc):
    r = MOE_CHUNK
    e_flat = idx2.reshape(-1)
    npairs = e_flat.shape[0]
    onehot = (e_flat[:, None] == jnp.arange(NE, dtype=I32)[None, :]).astype(I32)
    csum = jnp.cumsum(onehot, axis=0)
    rank = jnp.sum(onehot * csum, axis=1) - 1
    counts = csum[-1]
    nch = (counts + r - 1) // r
    cstart = jnp.cumsum(nch) - nch
    n_used = jnp.sum(nch)
    chunk = jnp.sum(onehot * cstart[None, :], axis=1) + rank // r
    pos = chunk * r + rank % r
    src = jnp.zeros((nc * r,), I32).at[pos].set(jnp.arange(npairs, dtype=I32) // 2)
    cid = jnp.minimum(jnp.arange(nc, dtype=I32), n_used - 1)
    ce = jnp.sum((cid[:, None] >= cstart[None, :]).astype(I32), axis=1) - 1
    rows = jnp.clip(counts[ce] - (cid - cstart[ce]) * r, 0, r)
    rows = jnp.where(jnp.arange(nc) < n_used, rows, 0)
    ngran = (rows + MOE_GRAN - 1) // MOE_GRAN
    return pos, src, cid, ce, ngran


def _gather_kernel(src_ref, ngran_ref, h_hbm, o_ref, buf, sem):
    s = pl.program_id(0)
    ns = pl.num_programs(0)
    per_chunk = MOE_CHUNK // MOE_SUB

    def live(t):
        return (t % per_chunk) * (MOE_SUB // MOE_GRAN) < ngran_ref[t // per_chunk]

    def row_copy(slot, r, t):
        return pltpu.make_async_copy(h_hbm.at[pl.ds(t, 1)], buf.at[slot, pl.ds(r, 1)], sem.at[slot])

    def start_tile(t, slot):
        def issue(r, c):
            row_copy(slot, r, src_ref[t * MOE_SUB + r]).start()
            return c

        lax.fori_loop(0, MOE_SUB, issue, 0, unroll=8)

    @pl.when((s == 0) & live(0))
    def _():
        start_tile(0, 0)

    nxt = jnp.minimum(s + 1, ns - 1)

    @pl.when((s + 1 < ns) & live(nxt))
    def _():
        start_tile(nxt, nxt % 2)

    @pl.when(live(s))
    def _():
        slot = s % 2

        def drain(r, c):
            row_copy(slot, r, 0).wait()
            return c

        lax.fori_loop(0, MOE_SUB, drain, 0, unroll=8)
        o_ref[...] = buf[slot].astype(BF16)

    @pl.when(jnp.logical_not(live(s)))
    def _():
        o_ref[...] = jnp.zeros_like(o_ref)


def _gather_rows(h, src, ngran, nc):
    nrows = nc * MOE_CHUNK
    grid_spec = pltpu.PrefetchScalarGridSpec(
        num_scalar_prefetch=2,
        grid=(nrows // MOE_SUB,),
        in_specs=[pl.BlockSpec(memory_space=pl.ANY)],
        out_specs=pl.BlockSpec((MOE_SUB, D), lambda s, *_: (s, 0)),
        scratch_shapes=[pltpu.VMEM((2, MOE_SUB, D), F32), pltpu.SemaphoreType.DMA((2,))],
    )
    return pl.pallas_call(
        _gather_kernel,
        grid_spec=grid_spec,
        out_shape=jax.ShapeDtypeStruct((nrows, D), BF16),
        compiler_params=_cp(("arbitrary",), 32),
        name="moe_gather",
    )(src, ngran, h)


def _moe_ffn_kernel(cid_ref, ce_ref, ngran_ref, hs_ref, wg_ref, wu_ref, wd_ref, y_ref, wgb, wub, wdb):
    c = pl.program_id(0)
    f = pl.program_id(1)
    ngran = ngran_ref[c]

    @pl.when(f == 0)
    def _():
        y_ref[...] = jnp.zeros_like(y_ref)

    def block(r0, rows):
        h = hs_ref[pl.ds(r0, rows), :]
        gg = jnp.dot(h, wgb[...], preferred_element_type=F32)
        uu = jnp.dot(h, wub[...], preferred_element_type=F32)
        a = (_silu(gg) * uu).astype(BF16)
        y_ref[pl.ds(r0, rows), :] += jnp.dot(a, wdb[...], preferred_element_type=F32)

    @pl.when(ngran > 0)
    def _():
        wgb[...] = wg_ref[...].astype(BF16)
        wub[...] = wu_ref[...].astype(BF16)
        wdb[...] = wd_ref[...].astype(BF16)

        per_body = MOE_BODY // MOE_GRAN
        nbody = ngran // per_body

        def body(s, carry):
            block(pl.multiple_of(s * MOE_BODY, MOE_BODY), MOE_BODY)
            return carry

        lax.fori_loop(0, nbody, body, 0)

        rem = ngran - nbody * per_body
        base = nbody * MOE_BODY
        rows = MOE_BODY // 2
        while rows >= MOE_GRAN:
            bit = rows // MOE_GRAN
            start = base + (rem // (2 * bit)) * (2 * rows)

            @pl.when((rem // bit) % 2 == 1)
            def _(start=start, rows=rows):
                block(pl.multiple_of(start, rows), rows)

            rows //= 2


def _moe_ffn(layer, hs, cid, ce, ngran, wg, wu, wd, nc, tf=256):
    dffe = wg.shape[3]
    nf = dffe // tf
    r = MOE_CHUNK

    def fidx(c, f, ngran):
        return jnp.where(ngran[c] > 0, f, nf - 1)

    grid_spec = pltpu.PrefetchScalarGridSpec(
        num_scalar_prefetch=3,
        grid=(nc, nf),
        in_specs=[pl.BlockSpec((r, D), lambda c, f, cid, ce, ns: (cid[c], 0)),
                  pl.BlockSpec((None, None, D, tf), lambda c, f, cid, ce, ns: (layer, ce[c], 0, fidx(c, f, ns))),
                  pl.BlockSpec((None, None, D, tf), lambda c, f, cid, ce, ns: (layer, ce[c], 0, fidx(c, f, ns))),
                  pl.BlockSpec((None, None, tf, D), lambda c, f, cid, ce, ns: (layer, ce[c], fidx(c, f, ns), 0))],
        out_specs=pl.BlockSpec((r, D), lambda c, f, cid, ce, ns: (c, 0)),
        scratch_shapes=[pltpu.VMEM((D, tf), BF16), pltpu.VMEM((D, tf), BF16), pltpu.VMEM((tf, D), BF16)],
    )
    return pl.pallas_call(
        _moe_ffn_kernel,
        grid_spec=grid_spec,
        out_shape=jax.ShapeDtypeStruct((nc * r, D), F32),
        compiler_params=_cp(("arbitrary", "arbitrary"), 56),
        name="moe_ffn",
    )(cid, ce, ngran, hs, wg, wu, wd)


def _combine_kernel(p1_ref, p2_ref, y_hbm, xp_ref, xs_ref, gp_ref, gs_ref, wt_ref, op_ref, os_ref, buf, sem,
                    *, tt, np_tiles):
    i = pl.program_id(0)
    n = pl.num_programs(0)

    def row_copy(slot, k, r, t):
        return pltpu.make_async_copy(y_hbm.at[pl.ds(t, 1)], buf.at[slot, k, pl.ds(r, 1)], sem.at[slot])

    def start_tile(t, slot):
        def issue(r, c):
            row_copy(slot, 0, r, p1_ref[t * tt + r]).start()
            row_copy(slot, 1, r, p2_ref[t * tt + r]).start()
            return c

        lax.fori_loop(0, tt, issue, 0, unroll=8)

    @pl.when(i == 0)
    def _():
        start_tile(0, 0)

    @pl.when(i + 1 < n)
    def _():
        start_tile(i + 1, (i + 1) % 2)

    slot = i % 2

    def drain(r, c):
        row_copy(slot, 0, r, 0).wait()
        row_copy(slot, 1, r, 0).wait()
        return c

    lax.fori_loop(0, tt, drain, 0, unroll=8)
    wt = wt_ref[...]
    y = wt[:, 0:1] * buf[slot, 0] + wt[:, 1:2] * buf[slot, 1]

    @pl.when(i < np_tiles)
    def _():
        op_ref[...] = xp_ref[...] + gp_ref[...] * y

    @pl.when(i >= np_tiles)
    def _():
        os_ref[...] = xs_ref[...] + gs_ref[...] * y


def _combine(y, pos1, pos2, xp2d, xs2d, gate_p, gate_s, rpg, wt):
    tp, ts = xp2d.shape[0], xs2d.shape[0]
    tt = ts
    np_tiles = tp // tt
    pi = lambda i: jnp.minimum(i, np_tiles - 1)
    grid_spec = pltpu.PrefetchScalarGridSpec(
        num_scalar_prefetch=2,
        grid=(np_tiles + 1,),
        in_specs=[pl.BlockSpec(memory_space=pl.ANY),
                  pl.BlockSpec((tt, D), lambda i, *_: (pi(i), 0)),
                  pl.BlockSpec((tt, D), lambda i, *_: (0, 0)),
                  pl.BlockSpec((None, 1, D), lambda i, *_: ((pi(i) * tt) // rpg, 0, 0)),
                  pl.BlockSpec((None, tt, D), lambda i, *_: (0, 0, 0)),
                  pl.BlockSpec((tt, 128), lambda i, *_: (i, 0))],
        out_specs=[pl.BlockSpec((tt, D), lambda i, *_: (pi(i), 0)),
                   pl.BlockSpec((tt, D), lambda i, *_: (0, 0))],
        scratch_shapes=[pltpu.VMEM((2, 2, tt, D), F32), pltpu.SemaphoreType.DMA((2,))],
    )
    return pl.pallas_call(
        functools.partial(_combine_kernel, tt=tt, np_tiles=np_tiles),
        grid_spec=grid_spec,
        out_shape=[jax.ShapeDtypeStruct((tp, D), F32), jax.ShapeDtypeStruct((ts, D), F32)],
        compiler_params=_cp(("arbitrary",), 32),
        name="moe_combine",
    )(pos1, pos2, y, xp2d, xs2d, gate_p, gate_s, wt)


def _moe(layer, xp2d, xs2d, modp, mods, g, wr, wg, wu, wd):
    tp, ts = xp2d.shape[0], xs2d.shape[0]
    wr_pad = jnp.pad(wr, ((0, 0), (0, 128 - NE))).astype(BF16)
    hp, ip, wtp = _router(xp2d, g, modp["sc2"], modp["sh2"], modp["rpg"], 1, wr_pad, 256)
    hsm, is_, wts = _router(xs2d, g, mods["sc2"], mods["sh2"], ts, ts, wr_pad, ts)
    h_all = jnp.concatenate([hp, hsm], axis=0)
    idx = jnp.concatenate([ip, is_], axis=0)
    wt = jnp.concatenate([wtp, wts], axis=0)
    nc = (2 * (tp + ts)) // MOE_CHUNK + NE
    pos, src, cid, ce, ngran = _moe_plan(idx[:, :2], nc)
    hs = _gather_rows(h_all, src, ngran, nc)
    y = _moe_ffn(layer, hs, cid, ce, ngran, wg, wu, wd, nc)
    return _combine(y, pos[0::2], pos[1::2], xp2d, xs2d, modp["g2"], mods["g2"], modp["rpg"], wt)


def kernel(x_prompt, x_sample, cache_k, cache_v, cache_logf, page_table, c_prompt, c_sample, norm1_g, norm2_g, w_ada, b_ada, w_in, b_forget, q_norm_g, k_norm_g, sgu_ln_g, sgu_ln_b, sgu_w, sgu_b, w_branch_attn, w_branch_sgu, w_out, ffd_w_gate, ffd_w_up, ffd_w_down, w_router, moe_w_gate, moe_w_up, moe_w_down):
    nb, seq, _ = x_prompt.shape
    db, nd, _ = x_sample.shape
    depth, npool = cache_k.shape[:2]
    tp, ts = nb * seq, db * nd

    c_all = jnp.concatenate([c_prompt, c_sample], axis=0)
    mc = -(-c_all.shape[0] // 8) * 8
    mod_all = _ada(jnp.pad(c_all, ((0, mc - c_all.shape[0]), (0, 0))), w_ada, b_ada)

    cache_k2 = cache_k.reshape(depth, npool, CH * NH, DH)
    cache_v2 = cache_v.reshape(depth, npool, CH * NH, DH)
    rloc, tot = _logf_pages(cache_logf.reshape(depth, npool, CH * NH))

    cut = 3 * AW
    wp_all = jnp.concatenate([w_in[:, :, :cut].astype(BF16), w_in[:, :, cut + NH:].astype(BF16)], axis=2)
    wf_all = jnp.pad(w_in[:, :, cut:cut + NH], ((0, 0), (0, 0), (0, 128 - NH))).astype(BF16)
    wa_all, ws_all, wo_all = w_branch_attn.astype(BF16), w_branch_sgu.astype(BF16), w_out.astype(BF16)
    fwg, fwu, fwd = ffd_w_gate.astype(BF16), ffd_w_up.astype(BF16), ffd_w_down.astype(BF16)

    eye_seq = jnp.eye(db, dtype=F32)
    xp, xs = x_prompt.reshape(tp, D), x_sample.reshape(ts, D)
    outs = {k: [] for k in ("kp", "vp", "fp", "ks", "vs", "fs", "ss")}
    for l in range(depth):
        names = ("sh1", "sc1", "g1", "sh2", "sc2", "g2")
        modp = {n: mod_all[l, :nb, i * D:(i + 1) * D][:, None, :] for i, n in enumerate(names)}
        mods = {n: jnp.repeat(mod_all[l, nb:nb + db, i * D:(i + 1) * D], nd, axis=0)[None] for i, n in enumerate(names)}
        modp["rpg"] = seq

        bf = jnp.pad(b_forget[l], (0, 128 - NH))[None]
        par = (wp_all, wf_all, bf, q_norm_g[l][None], k_norm_g[l][None])
        lng, lnb = sgu_ln_g[l][None], sgu_ln_b[l][None]
        n1 = norm1_g[l][None]
        n2 = norm2_g[l][None]

        q, k, v, lf, rest = _inproj(l, xp, n1, modp["sc1"], modp["sh1"], seq, 1, *par, tm=1024,
                                    qscale=DH ** -0.5 * LOG2E)
        lf = lf[:, :NH]
        crow = _cumsum(lf.reshape(nb, seq, NH).transpose(0, 2, 1), LOG2E)
        o = _flash(q.reshape(nb, seq, AW), k.reshape(nb, seq, AW), v.reshape(nb, seq, AW),
                   crow.transpose(0, 2, 1), crow)
        bmat = jnp.broadcast_to(sgu_b[l][:, :, None], (NG, CH, CH))
        mg, _ = _mix(l, o.reshape(tp, AW), rest, sgu_w[l], bmat, lng, lnb, wa_all, ws_all, 512)
        xp = _outproj(l, mg, wo_all, xp, modp["g1"], seq, 1, 1024)
        outs["kp"].append(k.reshape(nb, seq, NH, DH))
        outs["vp"].append(v.reshape(nb, seq, NH, DH))
        outs["fp"].append(lf.reshape(nb, seq, NH))

        q, k, v, lf, rest = _inproj(l, xs, n1, mods["sc1"], mods["sh1"], ts, ts, *par, tm=ts, qscale=DH ** -0.5)
        lf = lf[:, :NH]
        q_hq = q.reshape(db, nd, NH, DH).transpose(0, 2, 1, 3).reshape(db, NH * nd, DH)
        o = _paged(l, page_table, q_hq, k.reshape(db, nd * NH, DH), v.reshape(db, nd * NH, DH),
                   lf.reshape(db, 1, nd * NH), cache_k2, cache_v2, rloc, tot)
        o = o.reshape(db, NH, nd, DH).transpose(0, 2, 1, 3).reshape(ts, AW)
        w4 = sgu_w[l][:, :nd, :nd]
        wmix = (eye_seq[None, :, None, :, None] * w4[:, None, :, None, :]).reshape(NG, ts, ts)
        bmat = jnp.broadcast_to(jnp.tile(sgu_b[l][:, :nd], (1, db))[:, :, None], (NG, CH, CH))
        mg, vsn = _mix(l, o, rest, wmix, bmat, lng, lnb, wa_all, ws_all, ts)
        xs = _outproj(l, mg, wo_all, xs, mods["g1"], ts, ts, ts)
        outs["ks"].append(k.reshape(db, nd, NH, DH))
        outs["vs"].append(v.reshape(db, nd, NH, DH))
        outs["fs"].append(lf.reshape(db, nd, NH))
        outs["ss"].append(vsn.reshape(db, nd, SW))

        j = l // 2
        if l % 2 == 0:
            xp = _ffn_dense(j, xp, n2, modp["sc2"], modp["sh2"], modp["g2"], seq, 1, fwg, fwu, fwd, tm=512,
                            vmem_mb=48)
            xs = _ffn_dense(j, xs, n2, mods["sc2"], mods["sh2"], mods["g2"], ts, ts, fwg, fwu, fwd, tm=ts,
                            vmem_mb=32)
        else:
            xp, xs = _moe(j, xp, xs, modp, mods, n2, w_router[j], moe_w_gate, moe_w_up, moe_w_down)

    st = lambda n: jnp.stack(outs[n])
    return (xp.reshape(nb, seq, D), xs.reshape(db, nd, D), st("kp"), st("vp"), st("fp"),
            st("ks"), st("vs"), st("fs"), st("ss"))
```
